```python
import jax, jax.numpy as jnp
from jax import lax
import numpy as np

D_MODEL = 2048
BATCH = 4
SEQ = 2048
DEPTH = 2
DEC_BATCH = 128
DEC_SEQ = 8
PAST_LEN = 16384
PAGE_SIZE = 128

N_META = 16
POOL_WINDOWS = (2, 4, 8, 16)
N_POOL_GROUPS = len(POOL_WINDOWS)
POOL_GROUP = D_MODEL // N_POOL_GROUPS
POOL_BUF = max(POOL_WINDOWS) - 1
HG_DK = 128
HG_HEADS = D_MODEL // HG_DK
HG_DV = D_MODEL // HG_HEADS
CHUNK = 16
N_EXPERTS = 32
TOP_K = 4
D_FF = D_MODEL
SWIGLU_LIMIT = 7.0
SWIGLU_ALPHA = 1.702
MOE_BLOCK = 128
ALPHA_DN = (2 * DEPTH) ** 0.25
BETA_DN = (8 * DEPTH) ** -0.25
N_POOL_LAYERS = (DEPTH + 1) // 2
N_HGRN_LAYERS = DEPTH // 2
LN_EPS = 1e-5
RMS_EPS = 1e-6

kernel_name = "hybrid_pool_hgrn2_moe_step"


def layer_norm(x, g, b):
    xf = x.astype(jnp.float32)
    mu = jnp.mean(xf, axis=-1, keepdims=True)
    var = jnp.mean(jnp.square(xf - mu), axis=-1, keepdims=True)
    return ((xf - mu) * lax.rsqrt(var + LN_EPS) * g + b).astype(x.dtype)


def pool_mix(x, prefix, pos0, pool_w, pool_scale):
    B, T, D = x.shape
    xe = jnp.concatenate([prefix.astype(jnp.float32), x.astype(jnp.float32)], axis=1)
    c = jnp.pad(jnp.cumsum(xe, axis=1), ((0, 0), (1, 0), (0, 0)))
    pos = pos0 + jnp.arange(T)
    means = []
    for g, w in enumerate(POOL_WINDOWS):
        lo, hi = g * POOL_GROUP, (g + 1) * POOL_GROUP
        s = c[:, POOL_BUF + 1:POOL_BUF + 1 + T, lo:hi] - c[:, POOL_BUF + 1 - w:POOL_BUF + 1 - w + T, lo:hi]
        cnt = jnp.minimum(w, pos + 1).astype(jnp.float32)
        means.append(s / cnt[None, :, None])
    d = jnp.concatenate(means, axis=-1) - x.astype(jnp.float32)
    d = d.reshape(B, T, N_POOL_GROUPS, POOL_GROUP)
    y = jnp.einsum("btgc,gce->btge", d, pool_w).reshape(B, T, D) * pool_scale
    return y.astype(x.dtype)


def hgrn2_mix(x, S0, w_in, lb, norm_g, w_out, pad_front):
    B, T, _ = x.shape
    HK, HV = HG_HEADS * HG_DK, HG_HEADS * HG_DV
    proj = (x @ w_in).astype(jnp.float32)
    q = jax.nn.silu(proj[..., :HK])
    f = lb + (1.0 - lb) * jax.nn.sigmoid(proj[..., HK:2 * HK])
    logf = jnp.log(f)
    k = 1.0 - f
    v = proj[..., 2 * HK:2 * HK + HV]
    gate = proj[..., 2 * HK + HV:]
    L = -(-T // CHUNK) * CHUNK
    P = L - T
    pad = ((0, 0), (P, 0), (0, 0)) if pad_front else ((0, 0), (0, P), (0, 0))
    q, k, v, logf = [jnp.pad(a, pad) for a in (q, k, v, logf)]
    N = L // CHUNK

    def heads(a, d):
        return a.reshape(B, N, CHUNK, HG_HEADS, d).transpose(0, 3, 1, 2, 4)

    q, k, logf, v = heads(q, HG_DK), heads(k, HG_DK), heads(logf, HG_DK), heads(v, HG_DV)
    b = jnp.cumsum(logf, axis=3)
    b_last = b[:, :, :, -1:, :]
    qe = q * jnp.exp(b)
    ke = k * jnp.exp(-b)
    kd = k * jnp.exp(b_last - b)
    causal = jnp.tril(jnp.ones((CHUNK, CHUNK), dtype=bool))
    A = jnp.where(causal, jnp.einsum("bhncd,bhnsd->bhncs", qe, ke), 0.0)
    o_intra = jnp.einsum("bhncs,bhnsv->bhncv", A, v)
    chunk_kv = jnp.einsum("bhncd,bhncv->bhndv", kd, v)
    decay = jnp.exp(b_last[:, :, :, 0, :])

    def step(S, inp):
        dcy, kv, qn = inp
        o = jnp.einsum("bhcd,bhdv->bhcv", qn, S)
        return dcy[..., None] * S + kv, o

    S_fin, o_inter = lax.scan(
        step, S0.astype(jnp.float32),
        (decay.transpose(2, 0, 1, 3), chunk_kv.transpose(2, 0, 1, 3, 4), qe.transpose(2, 0, 1, 3, 4)))
    o = o_intra + o_inter.transpose(1, 2, 0, 3, 4)
    o = o.transpose(0, 2, 3, 1, 4).reshape(B, L, HG_HEADS, HG_DV)
    o = o[:, P:] if pad_front else o[:, :T]
    o = o * lax.rsqrt(jnp.mean(jnp.square(o), axis=-1, keepdims=True) + RMS_EPS)
    o = o.reshape(B, T, HV) * norm_g * jax.nn.silu(gate)
    y = o @ w_out
    return y.astype(x.dtype), S_fin


def moe(x, router_w, router_b, w_gate, b_gate, w_up, b_up, w_down, b_down):
    N, D = x.shape
    logits = x.astype(jnp.float32) @ router_w.astype(jnp.float32) + router_b
    vals, idx = lax.top_k(logits, TOP_K)
    gates = jax.nn.softmax(vals, axis=-1)
    A = N * TOP_K
    e_flat = idx.reshape(A)
    tok_flat = jnp.repeat(jnp.arange(N, dtype=jnp.int32), TOP_K)
    g_flat = gates.reshape(A)
    order = jnp.argsort(e_flat, stable=True)
    e_sorted = e_flat[order]
    counts = jnp.bincount(e_flat, length=N_EXPERTS)
    offsets = jnp.cumsum(counts) - counts
    pcounts = ((counts + MOE_BLOCK - 1) // MOE_BLOCK) * MOE_BLOCK
    cum_pad = jnp.cumsum(pcounts)
    poffsets = cum_pad - pcounts
    dest = poffsets[e_sorted] + (jnp.arange(A) - offsets[e_sorted])
    n_blocks = -(-(A + N_EXPERTS * (MOE_BLOCK - 1)) // MOE_BLOCK)
    M = n_blocks * MOE_BLOCK
    tok_buf = jnp.full((M,), N, dtype=jnp.int32).at[dest].set(tok_flat[order])
    gate_buf = jnp.zeros((M,), jnp.float32).at[dest].set(g_flat[order])
    block_e = jnp.minimum(jnp.searchsorted(cum_pad, jnp.arange(n_blocks) * MOE_BLOCK, side="right"),
                          N_EXPERTS - 1)
    x_pad = jnp.concatenate([x, jnp.zeros((1, D), x.dtype)], axis=0)

    def expert_block(args):
        tok, gt, e = args
        xb = x_pad[tok]
        hg = jnp.minimum(xb @ w_gate[e] + b_gate[e], SWIGLU_LIMIT)
        hu = jnp.clip(xb @ w_up[e] + b_up[e], -SWIGLU_LIMIT, SWIGLU_LIMIT)
        h = (hu + 1.0) * hg * jax.nn.sigmoid(SWIGLU_ALPHA * hg)
        return ((h @ w_down[e] + b_down[e]) * gt[:, None].astype(xb.dtype)).astype(x.dtype)

    out = lax.map(expert_block, (tok_buf.reshape(n_blocks, MOE_BLOCK),
                                 gate_buf.reshape(n_blocks, MOE_BLOCK), block_e))
    y = jax.ops.segment_sum(out.reshape(M, D), tok_buf, num_segments=N + 1)
    return y[:N]


def setup_inputs(seed: int = 0) -> dict:
    key = jax.random.key(seed)
    ks = jax.random.split(key, 21)
    D = D_MODEL
    HK, HV = HG_HEADS * HG_DK, HG_HEADS * HG_DV

    def nrm(k, shape, s):
        return jax.random.normal(k, shape, jnp.float32) * s

    return {
        "x_prompt": nrm(ks[0], (BATCH, SEQ, D), 1.0),
        "x_sample": nrm(ks[1], (DEC_BATCH, DEC_SEQ, D), 1.0),
        "state_pool": nrm(ks[2], (N_POOL_LAYERS, DEC_BATCH, POOL_BUF, D), 1.0),
        "state_hgrn": nrm(ks[3], (N_HGRN_LAYERS, DEC_BATCH, HG_HEADS, HG_DK, HG_DV), 0.5),
        "meta_tokens": nrm(ks[4], (N_META, D), 1.0),
        "pool_w": nrm(ks[5], (N_POOL_LAYERS, N_POOL_GROUPS, POOL_GROUP, POOL_GROUP), POOL_GROUP ** -0.5 * BETA_DN),
        "pool_scale": 1.0 + nrm(ks[6], (N_POOL_LAYERS, D), 0.02),
        "hg_w_in": nrm(ks[7], (N_HGRN_LAYERS, D, 2 * HK + HV + D), D ** -0.5),
        "hg_lb": nrm(ks[8], (DEPTH, HK), 0.1),
        "hg_norm_g": 1.0 + nrm(ks[9], (N_HGRN_LAYERS, HV), 0.02),
        "hg_w_out": nrm(ks[10], (N_HGRN_LAYERS, HV, D), HV ** -0.5 * BETA_DN),
        "ln_g": 1.0 + nrm(ks[11], (DEPTH, 2, D), 0.02),
        "ln_b": nrm(ks[12], (DEPTH, 2, D), 0.02),
        "router_w": nrm(ks[13], (DEPTH, D, N_EXPERTS), D ** -0.5),
        "router_b": nrm(ks[14], (DEPTH, N_EXPERTS), 0.01),
        "w_gate": nrm(ks[15], (DEPTH, N_EXPERTS, D, D_FF), D ** -0.5),
        "b_gate": nrm(ks[16], (DEPTH, N_EXPERTS, D_FF), 0.02),
        "w_up": nrm(ks[17], (DEPTH, N_EXPERTS, D, D_FF), D ** -0.5),
        "b_up": nrm(ks[18], (DEPTH, N_EXPERTS, D_FF), 0.02),
        "w_down": nrm(ks[19], (DEPTH, N_EXPERTS, D_FF, D), D_FF ** -0.5 * BETA_DN),
        "b_down": nrm(ks[20], (DEPTH, N_EXPERTS, D), 0.02),
    }


def reference(x_prompt, x_sample, state_pool, state_hgrn, meta_tokens, pool_w, pool_scale,
              hg_w_in, hg_lb, hg_norm_g, hg_w_out, ln_g, ln_b, router_w, router_b,
              w_gate, b_gate, w_up, b_up, w_down, b_down):
    B = x_prompt.shape[0]
    D = D_MODEL
    xp = jnp.concatenate([jnp.broadcast_to(meta_tokens.astype(x_prompt.dtype)[None], (B, N_META, D)),
                          x_prompt], axis=1)
    xs = x_sample
    sm = jax.nn.softmax(hg_lb.astype(jnp.float32), axis=0)
    lb_all = jnp.cumsum(sm, axis=0) - sm[0:1]
    pool_new_p, pool_new_s, hg_new_p, hg_new_s = [], [], [], []
    n_prompt_tok = xp.shape[0] * xp.shape[1]
    for l in range(DEPTH):
        if l % 2 == 0:
            p = l // 2
            prefix_s = state_pool[p].astype(xs.dtype)
            mp = pool_mix(xp, jnp.zeros((B, POOL_BUF, D), xp.dtype), 0, pool_w[p], pool_scale[p])
            ms = pool_mix(xs, prefix_s, PAST_LEN, pool_w[p], pool_scale[p])
            pool_new_p.append(xp[:, -POOL_BUF:])
            pool_new_s.append(jnp.concatenate([prefix_s, xs], axis=1)[:, -POOL_BUF:])
        else:
            h = l // 2
            mp, Sp = hgrn2_mix(xp, jnp.zeros((B, HG_HEADS, HG_DK, HG_DV), jnp.float32),
                               hg_w_in[h], lb_all[l], hg_norm_g[h], hg_w_out[h], True)
            ms, Ss = hgrn2_mix(xs, state_hgrn[h], hg_w_in[h], lb_all[l], hg_norm_g[h], hg_w_out[h], False)
            hg_new_p.append(Sp)
            hg_new_s.append(Ss)
        xp = layer_norm(ALPHA_DN * xp + mp, ln_g[l, 0], ln_b[l, 0])
        xs = layer_norm(ALPHA_DN * xs + ms, ln_g[l, 0], ln_b[l, 0])
        flat = jnp.concatenate([xp.reshape(-1, D), xs.reshape(-1, D)], axis=0)
        ff = moe(flat, router_w[l], router_b[l], w_gate[l], b_gate[l], w_up[l], b_up[l],
                 w_down[l], b_down[l])
        flat = layer_norm(ALPHA_DN * flat + ff, ln_g[l, 1], ln_b[l, 1])
        xp = flat[:n_prompt_tok].reshape(xp.shape)
        xs = flat[n_prompt_tok:].reshape(xs.shape)
    y_prompt = xp[:, N_META:]
    return (y_prompt, xs, jnp.stack(pool_new_p), jnp.stack(pool_new_s),
            jnp.stack(hg_new_p), jnp.stack(hg_new_s))
```

```python
import functools

import jax
import jax.numpy as jnp
from jax import lax
from jax.experimental import pallas as pl
from jax.experimental.pallas import tpu as pltpu

N_META = 16
POOL_WINDOWS = (2, 4, 8, 16)
POOL_BUF = max(POOL_WINDOWS) - 1
PAST_LEN = 16384
HG_DK = 128
SUB = 16
TOP_K = 4
SWIGLU_LIMIT = 7.0
SWIGLU_ALPHA = 1.702
LN_EPS = 1e-5
RMS_EPS = 1e-6

ROW_TILE = 256
SB_CHUNKS = 6
FF_TILE = 512
N_FF_TILES = 4
REC_BLOCK = 64
DISPATCH_TILE = 640
COMBINE_TOKENS = 160
MIB = 1024 * 1024

F32 = jnp.float32
BF16 = jnp.bfloat16


def _cparams(sem, vmem_mib):
    return pltpu.CompilerParams(dimension_semantics=sem, vmem_limit_bytes=vmem_mib * MIB)


def _layer_norm(x, g, b):
    mu = jnp.mean(x, axis=-1, keepdims=True)
    xc = x - mu
    var = jnp.mean(xc * xc, axis=-1, keepdims=True)
    return xc * lax.rsqrt(var + LN_EPS) * g + b


def _route_rows(h, rw, rb):
    logits = jnp.dot(h, rw, preferred_element_type=F32, precision=lax.Precision.HIGHEST) + rb
    n_e = logits.shape[-1]
    lane = lax.broadcasted_iota(jnp.int32, logits.shape, 1)
    vals, ids = [], []
    l = logits
    for _ in range(TOP_K):
        m = jnp.max(l, axis=-1, keepdims=True)
        ix = jnp.min(jnp.where(l == m, lane, n_e), axis=-1, keepdims=True)
        vals.append(m)
        ids.append(ix)
        l = jnp.where(lane == ix, -jnp.inf, l)
    v = jnp.concatenate(vals, axis=1)
    e = jnp.exp(v - vals[0])
    gates = e / jnp.sum(e, axis=-1, keepdims=True)
    return jnp.concatenate(ids, axis=1), gates


def _mix_post(x, y, alpha, g_ref, b_ref, rw_ref, rb_ref, h_ref, idx_ref, gate_ref):
    h = _layer_norm(alpha * x + y, g_ref[...], b_ref[...])
    h_ref[...] = h
    ids, gates = _route_rows(h, rw_ref[...], rb_ref[...])
    idx_ref[...] = ids
    gate_ref[...] = gates


def _window_sum(e, w):
    s = e
    sh = 1
    while sh < w:
        s = s + pltpu.roll(s, sh, axis=0)
        sh *= 2
    return s


def _pool_prompt_kernel(x_ref, halo_ref, pw_ref, ps_ref, g_ref, b_ref, rw_ref, rb_ref,
                        h_ref, idx_ref, gate_ref, *, tt, alpha):
    i = pl.program_id(1)
    cur = x_ref[0]
    halo = jnp.where(i > 0, halo_ref[0], 0.0)
    ext = jnp.concatenate([halo, cur], axis=0)
    grp = cur.shape[1] // len(POOL_WINDOWS)
    pos = i * tt + lax.broadcasted_iota(jnp.int32, (tt, 1), 0)
    ys = []
    for gi, w in enumerate(POOL_WINDOWS):
        s = _window_sum(ext[:, gi * grp:(gi + 1) * grp], w)[16:]
        cnt = jnp.minimum(w, pos + 1).astype(F32)
        d = s / cnt - cur[:, gi * grp:(gi + 1) * grp]
        ys.append(jnp.dot(d, pw_ref[gi], preferred_element_type=F32))
    y = jnp.concatenate(ys, axis=1) * ps_ref[...]
    _mix_post(cur, y, alpha, g_ref, b_ref, rw_ref, rb_ref, h_ref, idx_ref, gate_ref)


def _pool_sample_kernel(x_ref, pw_ref, ps_ref, g_ref, b_ref, rw_ref, rb_ref,
                        h_ref, idx_ref, gate_ref, *, alpha, t_new):
    sb, te, d_model = x_ref.shape
    ext = x_ref[...].reshape(sb * te, d_model)
    grp = d_model // len(POOL_WINDOWS)
    cur = x_ref[:, 16:, :].reshape(sb * t_new, d_model)
    tpos = lax.broadcasted_iota(jnp.int32, (sb, t_new, 1), 1).reshape(sb * t_new, 1)
    ys = []
    for gi, w in enumerate(POOL_WINDOWS):
        s = _window_sum(ext[:, gi * grp:(gi + 1) * grp], w)
        s = s.reshape(sb, te, grp)[:, 16:, :].reshape(sb * t_new, grp)
        cnt = jnp.minimum(w, PAST_LEN + tpos + 1).astype(F32)
        d = s / cnt - cur[:, gi * grp:(gi + 1) * grp]
        ys.append(jnp.dot(d, pw_ref[gi], preferred_element_type=F32))
    y = jnp.concatenate(ys, axis=1) * ps_ref[...]
    _mix_post(cur, y, alpha, g_ref, b_ref, rw_ref, rb_ref, h_ref, idx_ref, gate_ref)


def _const_spec(shape):
    nd = len(shape)
    return pl.BlockSpec(shape, lambda *_: (0,) * nd)


def _pool_layer(xp, xs_ext, pool_w, pool_scale, ln_g, ln_b, router_w, router_b, alpha):
    b_p, t_p, d = xp.shape
    n_e = router_w.shape[1]
    tt = 688
    assert t_p % tt == 0 and tt % 16 == 0
    ps = pool_scale.reshape(1, d)
    g2, b2, rb2 = ln_g.reshape(1, d), ln_b.reshape(1, d), router_b.reshape(1, n_e)
    consts = (pool_w, ps, g2, b2, router_w, rb2)
    const_specs = [_const_spec(c.shape) for c in consts]
    n_p = b_p * t_p
    nt = t_p // tt
    hp, ip, gp = pl.pallas_call(
        functools.partial(_pool_prompt_kernel, tt=tt, alpha=alpha),
        grid=(b_p, nt),
        in_specs=[pl.BlockSpec((1, tt, d), lambda b, i: (b, i, 0)),
                  pl.BlockSpec((1, 16, d), lambda b, i: (b, jnp.maximum(i * (tt // 16) - 1, 0), 0))]
        + const_specs,
        out_specs=[pl.BlockSpec((tt, d), lambda b, i: (b * nt + i, 0)),
                   pl.BlockSpec((tt, TOP_K), lambda b, i: (b * nt + i, 0)),
                   pl.BlockSpec((tt, TOP_K), lambda b, i: (b * nt + i, 0))],
        out_shape=[jax.ShapeDtypeStruct((n_p, d), F32),
                   jax.ShapeDtypeStruct((n_p, TOP_K), jnp.int32),
                   jax.ShapeDtypeStruct((n_p, TOP_K), F32)],
        compiler_params=_cparams(("arbitrary", "arbitrary"), 56),
        name="pool_prompt",
    )(xp, xp, *consts)

    b_s, te, _ = xs_ext.shape
    t_new = te - 16
    sb = 16
    assert b_s % sb == 0
    n_s = b_s * t_new
    hs, is_, gs = pl.pallas_call(
        functools.partial(_pool_sample_kernel, alpha=alpha, t_new=t_new),
        grid=(b_s // sb,),
        in_specs=[pl.BlockSpec((sb, te, d), lambda i: (i, 0, 0))] + const_specs,
        out_specs=[pl.BlockSpec((sb * t_new, d), lambda i: (i, 0)),
                   pl.BlockSpec((sb * t_new, TOP_K), lambda i: (i, 0)),
                   pl.BlockSpec((sb * t_new, TOP_K), lambda i: (i, 0))],
        out_shape=[jax.ShapeDtypeStruct((n_s, d), F32),
                   jax.ShapeDtypeStruct((n_s, TOP_K), jnp.int32),
                   jax.ShapeDtypeStruct((n_s, TOP_K), F32)],
        compiler_params=_cparams(("arbitrary",), 40),
        name="pool_sample",
    )(xs_ext, *consts)
    return (jnp.concatenate([hp, hs], axis=0), jnp.concatenate([ip, is_], axis=0),
            jnp.concatenate([gp, gs], axis=0))


def _routing_tables(idx, n_experts):
    n, k = idx.shape
    a = n * k
    e_flat = idx.reshape(a)
    experts = jnp.arange(n_experts, dtype=jnp.int32)
    onehot = (e_flat[:, None] == experts[None, :]).astype(jnp.int32)
    csum = jnp.cumsum(onehot, axis=0)
    counts = csum[-1]
    pcounts = ((counts + ROW_TILE - 1) // ROW_TILE) * ROW_TILE
    pend = jnp.cumsum(pcounts)
    poff = pend - pcounts
    dest = jnp.sum(onehot * (csum - onehot + poff[None, :]), axis=1)

    m_rows = -(-(a + n_experts * (ROW_TILE - 1)) // ROW_TILE) * ROW_TILE
    n_pad = n_experts * (ROW_TILE - 1)
    n_pad_entries = -(-n_pad // DISPATCH_TILE) * DISPATCH_TILE
    j = jnp.arange(ROW_TILE - 1, dtype=jnp.int32)[None, :]
    trash = m_rows + jnp.arange(n_pad_entries, dtype=jnp.int32)
    pad_dest = jnp.where(j < (pcounts - counts)[:, None], (poff + counts)[:, None] + j,
                         trash[:n_pad].reshape(n_experts, ROW_TILE - 1)).reshape(n_pad)
    dest_all = jnp.concatenate([dest, pad_dest, trash[n_pad:]]).astype(jnp.int32)
    m_total = m_rows + n_pad_entries

    sb_rows = SB_CHUNKS * ROW_TILE
    s_max = n_experts + m_rows // sb_rows
    nsb = (pcounts + sb_rows - 1) // sb_rows
    sb_end = jnp.cumsum(nsb)
    n_used = sb_end[-1]
    s = jnp.arange(s_max, dtype=jnp.int32)
    s_eff = jnp.minimum(s, n_used - 1)
    e_of_s = jnp.sum((s_eff[:, None] >= sb_end[None, :]).astype(jnp.int32), axis=1)
    e_of_s = jnp.minimum(e_of_s, n_experts - 1)
    oh_s = (e_of_s[:, None] == experts[None, :]).astype(jnp.int32)
    local = s_eff - jnp.sum(oh_s * (sb_end - nsb)[None, :], axis=1)
    row0 = jnp.sum(oh_s * poff[None, :], axis=1) + local * sb_rows
    rows = jnp.clip(jnp.sum(oh_s * pcounts[None, :], axis=1) - local * sb_rows, 0, sb_rows)
    nch = jnp.where(s < n_used, rows // ROW_TILE, 0)
    return dict(dest=dest.astype(jnp.int32), dest_all=dest_all, m_total=m_total,
                sb_e=e_of_s.astype(jnp.int32), sb_row0=row0.astype(jnp.int32),
                sb_nch=nch.astype(jnp.int32), n_used=n_used.reshape(1).astype(jnp.int32),
                s_max=s_max, n_main=a)


def _dispatch_kernel(dest_ref, h_ref, z_ref, xs_ref, sem, *, n_main_steps):
    i = pl.program_id(0)
    base = i * DISPATCH_TILE

    def row_copy(src_ref, src_row, a):
        return pltpu.make_async_copy(src_ref.at[pl.ds(src_row, 1)],
                                     xs_ref.at[pl.ds(dest_ref[base + a], 1)], sem)

    @pl.when(i < n_main_steps)
    def _():
        tok0 = i * (DISPATCH_TILE // TOP_K)

        def issue(t, c):
            for k in range(TOP_K):
                row_copy(h_ref, tok0 + t, t * TOP_K + k).start()
            return c
        lax.fori_loop(0, DISPATCH_TILE // TOP_K, issue, 0)

    @pl.when(i >= n_main_steps)
    def _():
        def issue(a, c):
            row_copy(z_ref, 0, a).start()
            return c
        lax.fori_loop(0, DISPATCH_TILE, issue, 0)

    def drain(a, c):
        row_copy(z_ref, 0, a).wait()
        return c
    lax.fori_loop(0, DISPATCH_TILE, drain, 0)


def _dispatch(h, tabs):
    n, d = h.shape
    n_main = tabs["n_main"]
    assert n_main % DISPATCH_TILE == 0
    n_steps = tabs["dest_all"].shape[0] // DISPATCH_TILE
    zeros = jnp.zeros((8, d), F32)
    return pl.pallas_call(
        functools.partial(_dispatch_kernel, n_main_steps=n_main // DISPATCH_TILE),
        grid_spec=pltpu.PrefetchScalarGridSpec(
            num_scalar_prefetch=1, grid=(n_steps,),
            in_specs=[pl.BlockSpec(memory_space=pl.ANY), pl.BlockSpec(memory_space=pl.ANY)],
            out_specs=pl.BlockSpec(memory_space=pl.ANY),
            scratch_shapes=[pltpu.SemaphoreType.DMA(())]),
        out_shape=jax.ShapeDtypeStruct((tabs["m_total"], d), F32),
        compiler_params=_cparams(("arbitrary",), 16),
        name="moe_dispatch",
    )(tabs["dest_all"], h, zeros)


def _expert_kernel(sbe_ref, row0_ref, nch_ref, nused_ref, xs_ref, wg_ref, wu_ref, wd_ref,
                   bg_ref, bu_ref, bd_ref, ys_ref, x_scr, h_scr, y_scr, x_sem, y_sem):
    del sbe_ref, nused_ref
    s = pl.program_id(0)
    j = pl.program_id(1)
    nch = nch_ref[s]
    row0 = pl.multiple_of(row0_ref[s], ROW_TILE)
    tm = ROW_TILE

    def x_copy(c):
        return pltpu.make_async_copy(xs_ref.at[pl.ds(pl.multiple_of(row0 + c * tm, tm), tm)],
                                     x_scr.at[pl.ds(pl.multiple_of(c * tm, tm), tm)], x_sem)

    def y_copy(n, slot, c):
        return pltpu.make_async_copy(
            y_scr.at[slot, pl.ds(pl.multiple_of(c * tm, tm), tm)],
            ys_ref.at[pl.ds(pl.multiple_of(row0 + c * tm, tm), tm), pl.ds(n * FF_TILE, FF_TILE)],
            y_sem.at[slot])

    def for_chunks(fn):
        def body(c, carry):
            fn(c)
            return carry
        lax.fori_loop(0, nch, body, 0)

    @pl.when(nch > 0)
    def _():
        @pl.when(j == 0)
        def _():
            for_chunks(lambda c: x_copy(c).start())
            for_chunks(lambda c: x_copy(c).wait())

        @pl.when(j < N_FF_TILES)
        def _():
            def gate_up(c):
                x = x_scr[pl.ds(pl.multiple_of(c * tm, tm), tm), :]
                hg = jnp.dot(x, wg_ref[0], preferred_element_type=F32) + bg_ref[0]
                hu = jnp.dot(x, wu_ref[0], preferred_element_type=F32) + bu_ref[0]
                hg = jnp.minimum(hg, SWIGLU_LIMIT)
                hu = jnp.clip(hu, -SWIGLU_LIMIT, SWIGLU_LIMIT)
                act = (hu + 1.0) * hg * jax.nn.sigmoid(SWIGLU_ALPHA * hg)
                h_scr[j, pl.ds(pl.multiple_of(c * tm, tm), tm), :] = act.astype(BF16)
            for_chunks(gate_up)

        for n in range(N_FF_TILES):
            slot = n % 2

            @pl.when(j == N_FF_TILES + n)
            def _():
                def down(c):
                    rows = pl.ds(pl.multiple_of(c * tm, tm), tm)
                    acc = bd_ref[0] + jnp.zeros((tm, FF_TILE), F32)
                    for f in range(N_FF_TILES):
                        w = wd_ref[0, f * FF_TILE:(f + 1) * FF_TILE, :].astype(BF16)
                        acc = acc + jnp.dot(h_scr[f, rows, :], w, preferred_element_type=F32)
                    y_scr[slot, rows, :] = acc
                    y_copy(n, slot, c).start()
                for_chunks(down)
                if n > 0:
                    for_chunks(lambda c: y_copy(n - 1, 1 - slot, c).wait())
                if n == N_FF_TILES - 1:
                    for_chunks(lambda c: y_copy(n, slot, c).wait())


def _experts(xs, tabs, w_gate, b_gate, w_up, b_up, w_down, b_down):
    n_e, d, f = w_gate.shape
    assert f == N_FF_TILES * FF_TILE and d == N_FF_TILES * FF_TILE
    m_total = xs.shape[0]
    sb_rows = SB_CHUNKS * ROW_TILE
    last = N_FF_TILES - 1

    def up_map(s, j, sbe, r0, nch, nu):
        return (sbe[s], 0, jnp.where(s < nu[0], jnp.minimum(j, last), last))

    def down_map(s, j, sbe, r0, nch, nu):
        return (sbe[s], 0, jnp.where(s < nu[0], jnp.maximum(j - N_FF_TILES, 0), last))

    return pl.pallas_call(
        _expert_kernel,
        grid_spec=pltpu.PrefetchScalarGridSpec(
            num_scalar_prefetch=4, grid=(tabs["s_max"], 2 * N_FF_TILES),
            in_specs=[pl.BlockSpec(memory_space=pl.ANY),
                      pl.BlockSpec((1, d, FF_TILE), up_map),
                      pl.BlockSpec((1, d, FF_TILE), up_map),
                      pl.BlockSpec((1, f, FF_TILE), down_map),
                      pl.BlockSpec((1, 1, FF_TILE), up_map),
                      pl.BlockSpec((1, 1, FF_TILE), up_map),
                      pl.BlockSpec((1, 1, FF_TILE), down_map)],
            out_specs=pl.BlockSpec(memory_space=pl.ANY),
            scratch_shapes=[pltpu.VMEM((sb_rows, d), F32),
                            pltpu.VMEM((N_FF_TILES, sb_rows, FF_TILE), BF16),
                            pltpu.VMEM((2, sb_rows, FF_TILE), F32),
                            pltpu.SemaphoreType.DMA(()),
                            pltpu.SemaphoreType.DMA((2,))]),
        out_shape=jax.ShapeDtypeStruct((m_total, d), F32),
        compiler_params=_cparams(("arbitrary", "arbitrary"), 56),
        name="moe_experts",
    )(tabs["sb_e"], tabs["sb_row0"], tabs["sb_nch"], tabs["n_used"], xs, w_gate, w_up, w_down,
      b_gate.reshape(n_e, 1, f), b_up.reshape(n_e, 1, f), b_down.reshape(n_e, 1, d))


def _combine_kernel(dest_ref, ys_ref, h_ref, gate_ref, g_ref, b_ref, o_ref, buf, sem, *, alpha):
    i = pl.program_id(0)
    n_steps = pl.num_programs(0)
    tt = COMBINE_TOKENS

    def row_copy(step, slot, t, k):
        return pltpu.make_async_copy(
            ys_ref.at[pl.ds(dest_ref[(step * tt + t) * TOP_K + k], 1)],
            buf.at[slot, k, pl.ds(t, 1)], sem.at[slot])

    def issue(step, slot):
        def body(t, c):
            for k in range(TOP_K):
                row_copy(step, slot, t, k).start()
            return c
        lax.fori_loop(0, tt, body, 0)

    @pl.when(i == 0)
    def _():
        issue(0, 0)

    @pl.when(i + 1 < n_steps)
    def _():
        issue(i + 1, (i + 1) % 2)

    slot = i % 2

    def drain(t, c):
        for k in range(TOP_K):
            row_copy(i, slot, t, k).wait()
        return c
    lax.fori_loop(0, tt, drain, 0)

    gates = gate_ref[...]
    y = gates[:, 0:1] * buf[slot, 0]
    for k in range(1, TOP_K):
        y = y + gates[:, k:k + 1] * buf[slot, k]
    o_ref[...] = _layer_norm(alpha * h_ref[...] + y, g_ref[...], b_ref[...])


def _combine(ys, dest, h, gates, ln_g, ln_b, alpha):
    n, d = h.shape
    tt = COMBINE_TOKENS
    assert n % tt == 0
    return pl.pallas_call(
        functools.partial(_combine_kernel, alpha=alpha),
        grid_spec=pltpu.PrefetchScalarGridSpec(
            num_scalar_prefetch=1, grid=(n // tt,),
            in_specs=[pl.BlockSpec(memory_space=pl.ANY),
                      pl.BlockSpec((tt, d), lambda i, dr: (i, 0)),
                      pl.BlockSpec((tt, TOP_K), lambda i, dr: (i, 0)),
                      pl.BlockSpec((1, d), lambda i, dr: (0, 0)),
                      pl.BlockSpec((1, d), lambda i, dr: (0, 0))],
            out_specs=pl.BlockSpec((tt, d), lambda i, dr: (i, 0)),
            scratch_shapes=[pltpu.VMEM((2, TOP_K, tt, d), F32),
                            pltpu.SemaphoreType.DMA((2,))]),
        out_shape=jax.ShapeDtypeStruct((n, d), F32),
        compiler_params=_cparams(("arbitrary",), 40),
        name="moe_combine",
    )(dest, ys, h, gates, ln_g.reshape(1, d), ln_b.reshape(1, d))


def _moe_layer(h, idx, gates, ln_g, ln_b, w_gate, b_gate, w_up, b_up, w_down, b_down, alpha):
    tabs = _routing_tables(idx, w_gate.shape[0])
    xs = _dispatch(h, tabs)
    ys = _experts(xs, tabs, w_gate, b_gate, w_up, b_up, w_down, b_down)
    return _combine(ys, tabs["dest"], h, gates, ln_g, ln_b, alpha)


def _proj_kernel(x_ref, w_ref, lb_ref, o_ref, *, tiles_per_section):
    sec = pl.program_id(1) // tiles_per_section
    p = jnp.dot(x_ref[...], w_ref[...], preferred_element_type=F32)
    sig = jax.nn.sigmoid(p)
    lb = lb_ref[...]
    forget = lb + (1.0 - lb) * sig
    o_ref[...] = jnp.where(sec == 1, forget, jnp.where(sec == 2, p, p * sig))


def _hgrn_proj(h, w_in, lb):
    n, d = h.shape
    tn = 512
    tm = 1160
    assert n % tm == 0 and d % tn == 0 and w_in.shape[1] == 4 * d
    tps = d // tn
    return pl.pallas_call(
        functools.partial(_proj_kernel, tiles_per_section=tps),
        grid=(n // tm, 4 * tps),
        in_specs=[pl.BlockSpec((tm, d), lambda i, j: (i, 0)),
                  pl.BlockSpec((d, tn), lambda i, j: (0, j)),
                  pl.BlockSpec((1, tn), lambda i, j: (0, j % tps))],
        out_specs=pl.BlockSpec((tm, tn), lambda i, j: (i, j)),
        out_shape=jax.ShapeDtypeStruct((n, 4 * d), F32),
        compiler_params=_cparams(("arbitrary", "arbitrary"), 48),
        name="hgrn_proj",
    )(h, w_in, lb.reshape(1, d))


_NT = (((1,), (1,)), ((), ()))
_TN = (((0,), (0,)), ((), ()))


def _hgrn_block(q, f, v, st):
    c = q.shape[0]
    n_sub = c // SUB
    lf = jnp.log(f)
    k = 1.0 - f
    row = lax.broadcasted_iota(jnp.int32, (c, 1), 0)
    loc = row % SUB
    bl = lf
    for sh in (1, 2, 4, 8):
        bl = bl + jnp.where(loc >= sh, pltpu.roll(bl, sh, axis=0), 0.0)
    tot = [bl[SUB * i + SUB - 1:SUB * i + SUB, :] for i in range(n_sub)]
    off = [jnp.zeros_like(tot[0])]
    for i in range(1, n_sub):
        off.append(off[-1] + tot[i - 1])
    b_last = off[-1] + tot[-1]
    if n_sub > 1:
        blk = row // SUB
        offs = off[0]
        tots = tot[0]
        for i in range(1, n_sub):
            offs = jnp.where(blk == i, off[i], offs)
            tots = jnp.where(blk == i, tot[i], tots)
        b = bl + offs
    else:
        b = bl
        tots = tot[0]
    r2 = lax.broadcasted_iota(jnp.int32, (c, c), 0)
    c2 = lax.broadcasted_iota(jnp.int32, (c, c), 1)
    a_loc = lax.dot_general(q * jnp.exp(bl), k * jnp.exp(-bl), _NT, preferred_element_type=F32)
    a = jnp.where((r2 // SUB == c2 // SUB) & (c2 <= r2), a_loc, 0.0)
    if n_sub > 1:
        k_end = k * jnp.exp(tots - bl)
        for jb in range(n_sub - 1):
            gamma = off[jb] + tot[jb]
            q_rel = q * jnp.exp(jnp.minimum(b - gamma, 0.0))
            a_j = lax.dot_general(q_rel, k_end, _NT, preferred_element_type=F32)
            a = jnp.where((c2 // SUB == jb) & (r2 // SUB > jb), a_j, a)
    o = jnp.dot(a, v, preferred_element_type=F32)
    o = o + lax.dot_general(q * jnp.exp(b), st, _NT, preferred_element_type=F32)
    kd = k * jnp.exp(b_last - b)
    st_new = st * jnp.exp(b_last) + lax.dot_general(v, kd, _TN, preferred_element_type=F32)
    o = o * lax.rsqrt(jnp.mean(o * o, axis=-1, keepdims=True) + RMS_EPS)
    return o, st_new


def _rec_prompt_kernel(q_ref, f_ref, v_ref, o_ref, s_ref, st_scr, *, n_blocks):
    st_scr[...] = jnp.zeros_like(st_scr)

    def run(r0, c):
        rows = pl.ds(r0, c)
        o, st = _hgrn_block(q_ref[rows, :], f_ref[rows, :], v_ref[rows, :], st_scr[...])
        o_ref[rows, :] = o
        st_scr[...] = st

    run(0, SUB)

    def body(i, carry):
        run(pl.multiple_of(SUB + i * REC_BLOCK, SUB), REC_BLOCK)
        return carry
    lax.fori_loop(0, n_blocks, body, 0)
    s_ref[0, 0] = st_scr[...].T


def _rec_sample_kernel(q_ref, f_ref, v_ref, s0_ref, o_ref, s_ref, *, t_new):
    sb = s0_ref.shape[1]

    def body(i, carry):
        rows = pl.ds(pl.multiple_of(i * t_new, t_new), t_new)
        o, st = _hgrn_block_short(q_ref[rows, :], f_ref[rows, :], v_ref[rows, :], s0_ref[0, i, 0].T)
        o_ref[rows, :] = o
        s_ref[0, i, 0] = st.T
        return carry
    lax.fori_loop(0, sb, body, 0)


def _hgrn_block_short(q, f, v, st):
    c = q.shape[0]
    lf = jnp.log(f)
    k = 1.0 - f
    row = lax.broadcasted_iota(jnp.int32, (c, 1), 0)
    bl = lf
    sh = 1
    while sh < c:
        bl = bl + jnp.where(row >= sh, pltpu.roll(bl, sh, axis=0), 0.0)
        sh *= 2
    b_last = bl[c - 1:c, :]
    r2 = lax.broadcasted_iota(jnp.int32, (c, c), 0)
    c2 = lax.broadcasted_iota(jnp.int32, (c, c), 1)
    qe = q * jnp.exp(bl)
    a = lax.dot_general(qe, k * jnp.exp(-bl), _NT, preferred_element_type=F32)
    a = jnp.where(c2 <= r2, a, 0.0)
    o = jnp.dot(a, v, preferred_element_type=F32)
    o = o + lax.dot_general(qe, st, _NT, preferred_element_type=F32)
    kd = k * jnp.exp(b_last - bl)
    st_new = st * jnp.exp(b_last) + lax.dot_general(v, kd, _TN, preferred_element_type=F32)
    o = o * lax.rsqrt(jnp.mean(o * o, axis=-1, keepdims=True) + RMS_EPS)
    return o, st_new


def _hgrn_recurrence(proj, state_s, b_p, t_p, b_s, t_s):
    n, d4 = proj.shape
    d = d4 // 4
    heads = d // HG_DK
    n_p = b_p * t_p
    assert (t_p - SUB) % REC_BLOCK == 0 and t_s < SUB and t_s % 8 == 0
    o_p, s_p = pl.pallas_call(
        functools.partial(_rec_prompt_kernel, n_blocks=(t_p - SUB) // REC_BLOCK),
        grid=(b_p, heads),
        in_specs=[pl.BlockSpec((t_p, HG_DK), lambda b, h: (b, h)),
                  pl.BlockSpec((t_p, HG_DK), lambda b, h: (b, heads + h)),
                  pl.BlockSpec((t_p, HG_DK), lambda b, h: (b, 2 * heads + h))],
        out_specs=[pl.BlockSpec((t_p, HG_DK), lambda b, h: (b, h)),
                   pl.BlockSpec((1, 1, HG_DK, HG_DK), lambda b, h: (b, h, 0, 0))],
        out_shape=[jax.ShapeDtypeStruct((n_p, d), F32),
                   jax.ShapeDtypeStruct((b_p, heads, HG_DK, HG_DK), F32)],
        scratch_shapes=[pltpu.VMEM((HG_DK, HG_DK), F32)],
        compiler_params=_cparams(("arbitrary", "arbitrary"), 32),
        name="hgrn_rec_prompt",
    )(proj, proj, proj)

    proj_s = proj[n_p:]
    sb = 32
    assert b_s % sb == 0
    o_s, s_s = pl.pallas_call(
        functools.partial(_rec_sample_kernel, t_new=t_s),
        grid=(heads, b_s // sb),
        in_specs=[pl.BlockSpec((sb * t_s, HG_DK), lambda h, i: (i, h)),
                  pl.BlockSpec((sb * t_s, HG_DK), lambda h, i: (i, heads + h)),
                  pl.BlockSpec((sb * t_s, HG_DK), lambda h, i: (i, 2 * heads + h)),
                  pl.BlockSpec((1, sb, 1, HG_DK, HG_DK), lambda h, i: (0, i, h, 0, 0))],
        out_specs=[pl.BlockSpec((sb * t_s, HG_DK), lambda h, i: (i, h)),
                   pl.BlockSpec((1, sb, 1, HG_DK, HG_DK), lambda h, i: (0, i, h, 0, 0))],
        out_shape=[jax.ShapeDtypeStruct((b_s * t_s, d), F32),
                   jax.ShapeDtypeStruct((1, b_s, heads, HG_DK, HG_DK), F32)],
        compiler_params=_cparams(("arbitrary", "arbitrary"), 32),
        name="hgrn_rec_sample",
    )(proj_s, proj_s, proj_s, state_s)
    return jnp.concatenate([o_p, o_s], axis=0), s_p, s_s


def _hgrn_out_kernel(o_ref, sg_ref, ng_ref, w_ref, x_ref, g_ref, b_ref, rw_ref, rb_ref,
                     h_ref, idx_ref, gate_ref, *, alpha):
    z = (o_ref[...] * ng_ref[...] * sg_ref[...]).astype(BF16)
    y = jnp.dot(z, w_ref[...], preferred_element_type=F32)
    _mix_post(x_ref[...], y, alpha, g_ref, b_ref, rw_ref, rb_ref, h_ref, idx_ref, gate_ref)


def _hgrn_out(o, proj, norm_g, w_out, x, ln_g, ln_b, router_w, router_b, alpha):
    n, d = x.shape
    n_e = router_w.shape[1]
    tm = 464
    assert n % tm == 0
    consts = (ln_g.reshape(1, d), ln_b.reshape(1, d), router_w, router_b.reshape(1, n_e))
    row = lambda i: (i, 0)
    return pl.pallas_call(
        functools.partial(_hgrn_out_kernel, alpha=alpha),
        grid=(n // tm,),
        in_specs=[pl.BlockSpec((tm, d), row),
                  pl.BlockSpec((tm, d), lambda i: (i, 3)),
                  _const_spec((1, d)),
                  _const_spec((d, d)),
                  pl.BlockSpec((tm, d), row)] + [_const_spec(c.shape) for c in consts],
        out_specs=[pl.BlockSpec((tm, d), row), pl.BlockSpec((tm, TOP_K), row),
                   pl.BlockSpec((tm, TOP_K), row)],
        out_shape=[jax.ShapeDtypeStruct((n, d), F32),
                   jax.ShapeDtypeStruct((n, TOP_K), jnp.int32),
                   jax.ShapeDtypeStruct((n, TOP_K), F32)],
        compiler_params=_cparams(("arbitrary",), 56),
        name="hgrn_out",
    )(o, proj, norm_g.reshape(1, d), w_out.astype(BF16), x, *consts)


def kernel(x_prompt, x_sample, state_pool, state_hgrn, meta_tokens, pool_w, pool_scale, hg_w_in, hg_lb, hg_norm_g, hg_w_out, ln_g, ln_b, router_w, router_b, w_gate, b_gate, w_up, b_up, w_down, b_down):
    depth = ln_g.shape[0]
    assert depth == 2, "layer 0 = pooling mixer, layer 1 = HGRN2 mixer"
    alpha = float((2 * depth) ** 0.25)
    b_p, seq, d = x_prompt.shape
    b_s, t_s, _ = x_sample.shape
    t_p = seq + N_META
    n_p = b_p * t_p

    xp = jnp.concatenate([jnp.broadcast_to(meta_tokens[None], (b_p, N_META, d)), x_prompt], axis=1)
    prefix = state_pool[0]
    xs_ext = jnp.concatenate([jnp.zeros((b_s, 16 - POOL_BUF, d), F32), prefix, x_sample], axis=1)

    h, idx, gates = _pool_layer(xp, xs_ext, pool_w[0], pool_scale[0], ln_g[0, 0], ln_b[0, 0],
                                router_w[0], router_b[0], alpha)
    h = _moe_layer(h, idx, gates, ln_g[0, 1], ln_b[0, 1], w_gate[0], b_gate[0], w_up[0], b_up[0],
                   w_down[0], b_down[0], alpha)

    sm = jax.nn.softmax(hg_lb.astype(F32), axis=0)
    lb = (jnp.cumsum(sm, axis=0) - sm[0:1])[1]
    proj = _hgrn_proj(h, hg_w_in[0], lb)
    o, s_p, s_s = _hgrn_recurrence(proj, state_hgrn, b_p, t_p, b_s, t_s)
    h, idx, gates = _hgrn_out(o, proj, hg_norm_g[0], hg_w_out[0], h, ln_g[1, 0], ln_b[1, 0],
                              router_w[1], router_b[1], alpha)
    h = _moe_layer(h, idx, gates, ln_g[1, 1], ln_b[1, 1], w_gate[1], b_gate[1], w_up[1], b_up[1],
                   w_down[1], b_down[1], alpha)

    y_prompt = h[:n_p].reshape(b_p, t_p, d)[:, N_META:]
    y_sample = h[n_p:].reshape(b_s, t_s, d)
    pool_p = x_prompt[:, seq - POOL_BUF:][None]
    pool_s = jnp.concatenate([prefix, x_sample], axis=1)[:, t_s:][None]
    return (y_prompt, y_sample, pool_p, pool_s, s_p[None], s_s)
```

```python
import functools

import jax
import jax.numpy as jnp
from jax import lax
from jax.experimental import pallas as pl
from jax.experimental.pallas import tpu as pltpu

N_META = 16
POOL_WINDOWS = (2, 4, 8, 16)
POOL_BUF = max(POOL_WINDOWS) - 1
PAST_LEN = 16384
HG_DK = 128
SUB = 16
TOP_K = 4
SWIGLU_LIMIT = 7.0
SWIGLU_ALPHA = 1.702
LN_EPS = 1e-5
RMS_EPS = 1e-6

ROW_TILE = 256
SB_CHUNKS = 6
FF_TILE = 512
N_FF_TILES = 4
REC_BLOCK = 64
REC_HEADS = 4
REC_SEQS = 8
DISPATCH_TILE = 640
COMBINE_TOKENS = 160
MIB = 1024 * 1024

F32 = jnp.float32
BF16 = jnp.bfloat16


def _cparams(sem, vmem_mib):
    return pltpu.CompilerParams(dimension_semantics=sem, vmem_limit_bytes=vmem_mib * MIB)


def _layer_norm(x, g, b):
    mu = jnp.mean(x, axis=-1, keepdims=True)
    xc = x - mu
    var = jnp.mean(xc * xc, axis=-1, keepdims=True)
    return xc * lax.rsqrt(var + LN_EPS) * g + b


def _route_rows(h, rw, rb):
    logits = jnp.dot(h, rw, preferred_element_type=F32, precision=lax.Precision.HIGHEST) + rb
    n_e = logits.shape[-1]
    lane = lax.broadcasted_iota(jnp.int32, logits.shape, 1)
    vals, ids = [], []
    l = logits
    for _ in range(TOP_K):
        m = jnp.max(l, axis=-1, keepdims=True)
        ix = jnp.min(jnp.where(l == m, lane, n_e), axis=-1, keepdims=True)
        vals.append(m)
        ids.append(ix)
        l = jnp.where(lane == ix, -jnp.inf, l)
    v = jnp.concatenate(vals, axis=1)
    e = jnp.exp(v - vals[0])
    gates = e / jnp.sum(e, axis=-1, keepdims=True)
    return jnp.concatenate(ids, axis=1), gates


def _mix_post(x, y, alpha, g_ref, b_ref, rw_ref, rb_ref, h_ref, idx_ref, gate_ref):
    h = _layer_norm(alpha * x + y, g_ref[...], b_ref[...])
    h_ref[...] = h
    ids, gates = _route_rows(h, rw_ref[...], rb_ref[...])
    idx_ref[...] = ids
    gate_ref[...] = gates


def _window_sum(e, w):
    s = e
    sh = 1
    while sh < w:
        s = s + pltpu.roll(s, sh, axis=0)
        sh *= 2
    return s


def _pool_prompt_kernel(x_ref, halo_ref, pw_ref, ps_ref, g_ref, b_ref, rw_ref, rb_ref,
                        h_ref, idx_ref, gate_ref, *, tt, alpha):
    i = pl.program_id(1)
    cur = x_ref[0]
    halo = jnp.where(i > 0, halo_ref[0], 0.0)
    ext = jnp.concatenate([halo, cur], axis=0)
    grp = cur.shape[1] // len(POOL_WINDOWS)
    pos = i * tt + lax.broadcasted_iota(jnp.int32, (tt, 1), 0)
    ys = []
    for gi, w in enumerate(POOL_WINDOWS):
        s = _window_sum(ext[:, gi * grp:(gi + 1) * grp], w)[16:]
        cnt = jnp.minimum(w, pos + 1).astype(F32)
        d = s / cnt - cur[:, gi * grp:(gi + 1) * grp]
        ys.append(jnp.dot(d, pw_ref[gi], preferred_element_type=F32, precision=lax.Precision.HIGHEST))
    y = jnp.concatenate(ys, axis=1) * ps_ref[...]
    _mix_post(cur, y, alpha, g_ref, b_ref, rw_ref, rb_ref, h_ref, idx_ref, gate_ref)


def _pool_sample_kernel(x_ref, pw_ref, ps_ref, g_ref, b_ref, rw_ref, rb_ref,
                        h_ref, idx_ref, gate_ref, *, alpha, t_new):
    sb, te, d_model = x_ref.shape
    ext = x_ref[...].reshape(sb * te, d_model)
    grp = d_model // len(POOL_WINDOWS)
    cur = x_ref[:, 16:, :].reshape(sb * t_new, d_model)
    tpos = lax.broadcasted_iota(jnp.int32, (sb, t_new, 1), 1).reshape(sb * t_new, 1)
    ys = []
    for gi, w in enumerate(POOL_WINDOWS):
        s = _window_sum(ext[:, gi * grp:(gi + 1) * grp], w)
        s = s.reshape(sb, te, grp)[:, 16:, :].reshape(sb * t_new, grp)
        cnt = jnp.minimum(w, PAST_LEN + tpos + 1).astype(F32)
        d = s / cnt - cur[:, gi * grp:(gi + 1) * grp]
        ys.append(jnp.dot(d, pw_ref[gi], preferred_element_type=F32, precision=lax.Precision.HIGHEST))
    y = jnp.concatenate(ys, axis=1) * ps_ref[...]
    _mix_post(cur, y, alpha, g_ref, b_ref, rw_ref, rb_ref, h_ref, idx_ref, gate_ref)


def _const_spec(shape):
    nd = len(shape)
    return pl.BlockSpec(shape, lambda *_: (0,) * nd)


def _pool_layer(xp, xs_ext, pool_w, pool_scale, ln_g, ln_b, router_w, router_b, alpha):
    b_p, t_p, d = xp.shape
    n_e = router_w.shape[1]
    tt = 688
    assert t_p % tt == 0 and tt % 16 == 0
    ps = pool_scale.reshape(1, d)
    g2, b2, rb2 = ln_g.reshape(1, d), ln_b.reshape(1, d), router_b.reshape(1, n_e)
    consts = (pool_w, ps, g2, b2, router_w, rb2)
    const_specs = [_const_spec(c.shape) for c in consts]
    n_p = b_p * t_p
    nt = t_p // tt
    hp, ip, gp = pl.pallas_call(
        functools.partial(_pool_prompt_kernel, tt=tt, alpha=alpha),
        grid=(b_p, nt),
        in_specs=[pl.BlockSpec((1, tt, d), lambda b, i: (b, i, 0)),
                  pl.BlockSpec((1, 16, d), lambda b, i: (b, jnp.maximum(i * (tt // 16) - 1, 0), 0))]
        + const_specs,
        out_specs=[pl.BlockSpec((tt, d), lambda b, i: (b * nt + i, 0)),
                   pl.BlockSpec((tt, TOP_K), lambda b, i: (b * nt + i, 0)),
                   pl.BlockSpec((tt, TOP_K), lambda b, i: (b * nt + i, 0))],
        out_shape=[jax.ShapeDtypeStruct((n_p, d), F32),
                   jax.ShapeDtypeStruct((n_p, TOP_K), jnp.int32),
                   jax.ShapeDtypeStruct((n_p, TOP_K), F32)],
        compiler_params=_cparams(("arbitrary", "arbitrary"), 56),
        name="pool_prompt",
    )(xp, xp, *consts)

    b_s, te, _ = xs_ext.shape
    t_new = te - 16
    sb = 16
    assert b_s % sb == 0
    n_s = b_s * t_new
    hs, is_, gs = pl.pallas_call(
        functools.partial(_pool_sample_kernel, alpha=alpha, t_new=t_new),
        grid=(b_s // sb,),
        in_specs=[pl.BlockSpec((sb, te, d), lambda i: (i, 0, 0))] + const_specs,
        out_specs=[pl.BlockSpec((sb * t_new, d), lambda i: (i, 0)),
                   pl.BlockSpec((sb * t_new, TOP_K), lambda i: (i, 0)),
                   pl.BlockSpec((sb * t_new, TOP_K), lambda i: (i, 0))],
        out_shape=[jax.ShapeDtypeStruct((n_s, d), F32),
                   jax.ShapeDtypeStruct((n_s, TOP_K), jnp.int32),
                   jax.ShapeDtypeStruct((n_s, TOP_K), F32)],
        compiler_params=_cparams(("arbitrary",), 40),
        name="pool_sample",
    )(xs_ext, *consts)
    return (jnp.concatenate([hp, hs], axis=0), jnp.concatenate([ip, is_], axis=0),
            jnp.concatenate([gp, gs], axis=0))


def _routing_tables(idx, n_experts):
    n, k = idx.shape
    a = n * k
    e_flat = idx.reshape(a)
    experts = jnp.arange(n_experts, dtype=jnp.int32)
    onehot = (e_flat[:, None] == experts[None, :]).astype(jnp.int32)
    csum = jnp.cumsum(onehot, axis=0)
    counts = csum[-1]
    pcounts = ((counts + ROW_TILE - 1) // ROW_TILE) * ROW_TILE
    pend = jnp.cumsum(pcounts)
    poff = pend - pcounts
    dest = jnp.sum(onehot * (csum - onehot + poff[None, :]), axis=1)

    m_rows = -(-(a + n_experts * (ROW_TILE - 1)) // ROW_TILE) * ROW_TILE
    n_pad = n_experts * (ROW_TILE - 1)
    n_pad_entries = -(-n_pad // DISPATCH_TILE) * DISPATCH_TILE
    j = jnp.arange(ROW_TILE - 1, dtype=jnp.int32)[None, :]
    trash = m_rows + jnp.arange(n_pad_entries, dtype=jnp.int32)
    pad_dest = jnp.where(j < (pcounts - counts)[:, None], (poff + counts)[:, None] + j,
                         trash[:n_pad].reshape(n_experts, ROW_TILE - 1)).reshape(n_pad)
    dest_all = jnp.concatenate([dest, pad_dest, trash[n_pad:]]).astype(jnp.int32)
    m_total = m_rows + n_pad_entries

    sb_rows = SB_CHUNKS * ROW_TILE
    s_max = n_experts + m_rows // sb_rows
    nsb = (pcounts + sb_rows - 1) // sb_rows
    sb_end = jnp.cumsum(nsb)
    n_used = sb_end[-1]
    s = jnp.arange(s_max, dtype=jnp.int32)
    s_eff = jnp.minimum(s, n_used - 1)
    e_of_s = jnp.sum((s_eff[:, None] >= sb_end[None, :]).astype(jnp.int32), axis=1)
    e_of_s = jnp.minimum(e_of_s, n_experts - 1)
    oh_s = (e_of_s[:, None] == experts[None, :]).astype(jnp.int32)
    local = s_eff - jnp.sum(oh_s * (sb_end - nsb)[None, :], axis=1)
    row0 = jnp.sum(oh_s * poff[None, :], axis=1) + local * sb_rows
    rows = jnp.clip(jnp.sum(oh_s * pcounts[None, :], axis=1) - local * sb_rows, 0, sb_rows)
    nch = jnp.where(s < n_used, rows // ROW_TILE, 0)
    return dict(dest=dest.astype(jnp.int32), dest_all=dest_all, m_total=m_total,
                sb_e=e_of_s.astype(jnp.int32), sb_row0=row0.astype(jnp.int32),
                sb_nch=nch.astype(jnp.int32), n_used=n_used.reshape(1).astype(jnp.int32),
                s_max=s_max, n_main=a)


def _dispatch_kernel(dest_ref, h_ref, xs_ref, z_scr, sem, *, n_main_steps):
    i = pl.program_id(0)
    base = i * DISPATCH_TILE

    def row_copy(src_ref, src_row, a):
        return pltpu.make_async_copy(src_ref.at[pl.ds(src_row, 1)],
                                     xs_ref.at[pl.ds(dest_ref[base + a], 1)], sem)

    @pl.when(i == 0)
    def _():
        z_scr[...] = jnp.zeros_like(z_scr)

    @pl.when(i < n_main_steps)
    def _():
        def issue(t, c):
            for k in range(TOP_K):
                row_copy(h_ref, t, t * TOP_K + k).start()
            return c
        lax.fori_loop(0, DISPATCH_TILE // TOP_K, issue, 0)

    @pl.when(i >= n_main_steps)
    def _():
        def issue(a, c):
            row_copy(z_scr, 0, a).start()
            return c
        lax.fori_loop(0, DISPATCH_TILE, issue, 0)

    all_rows = xs_ref.at[pl.ds(0, DISPATCH_TILE)]
    pltpu.make_async_copy(all_rows, all_rows, sem).wait()


def _dispatch(h, tabs):
    n, d = h.shape
    n_main = tabs["n_main"]
    assert n_main % DISPATCH_TILE == 0
    n_steps = tabs["dest_all"].shape[0] // DISPATCH_TILE
    n_main_steps = n_main // DISPATCH_TILE
    tt = DISPATCH_TILE // TOP_K
    return pl.pallas_call(
        functools.partial(_dispatch_kernel, n_main_steps=n_main_steps),
        grid_spec=pltpu.PrefetchScalarGridSpec(
            num_scalar_prefetch=1, grid=(n_steps,),
            in_specs=[pl.BlockSpec((tt, d), lambda i, dr: (jnp.minimum(i, n_main_steps - 1), 0))],
            out_specs=pl.BlockSpec(memory_space=pl.ANY),
            scratch_shapes=[pltpu.VMEM((8, d), F32), pltpu.SemaphoreType.DMA(())]),
        out_shape=jax.ShapeDtypeStruct((tabs["m_total"], d), F32),
        compiler_params=_cparams(("arbitrary",), 16),
        name="moe_dispatch",
    )(tabs["dest_all"], h)


def _expert_kernel(sbe_ref, row0_ref, nch_ref, nused_ref, xs_ref, wg_ref, wu_ref, wd_ref,
                   bg_ref, bu_ref, bd_ref, ys_ref, x_scr, h_scr, y_scr, x_sem, y_sem):
    del sbe_ref, nused_ref
    s = pl.program_id(0)
    j = pl.program_id(1)
    nch = nch_ref[s]
    row0 = pl.multiple_of(row0_ref[s], ROW_TILE)
    tm = ROW_TILE

    def x_copy(c):
        return pltpu.make_async_copy(xs_ref.at[pl.ds(pl.multiple_of(row0 + c * tm, tm), tm)],
                                     x_scr.at[pl.ds(pl.multiple_of(c * tm, tm), tm)], x_sem.at[c])

    def y_copy(n, slot, c):
        return pltpu.make_async_copy(
            y_scr.at[slot, pl.ds(pl.multiple_of(c * tm, tm), tm)],
            ys_ref.at[pl.ds(pl.multiple_of(row0 + c * tm, tm), tm), pl.ds(n * FF_TILE, FF_TILE)],
            y_sem.at[slot])

    def for_chunks(fn):
        def body(c, carry):
            fn(c)
            return carry
        lax.fori_loop(0, nch, body, 0)

    @pl.when(nch > 0)
    def _():
        @pl.when(j == 0)
        def _():
            for_chunks(lambda c: x_copy(c).start())

        @pl.when(j < N_FF_TILES)
        def _():
            def gate_up(c):
                @pl.when(j == 0)
                def _():
                    x_copy(c).wait()
                x = x_scr[pl.ds(pl.multiple_of(c * tm, tm), tm), :]
                hg = jnp.dot(x, wg_ref[0], preferred_element_type=F32) + bg_ref[0]
                hu = jnp.dot(x, wu_ref[0], preferred_element_type=F32) + bu_ref[0]
                hg = jnp.minimum(hg, SWIGLU_LIMIT)
                hu = jnp.clip(hu, -SWIGLU_LIMIT, SWIGLU_LIMIT)
                act = (hu + 1.0) * hg * jax.nn.sigmoid(SWIGLU_ALPHA * hg)
                h_scr[j, pl.ds(pl.multiple_of(c * tm, tm), tm), :] = act.astype(BF16)
            for_chunks(gate_up)

        for n in range(N_FF_TILES):
            slot = n % 2

            @pl.when(j == N_FF_TILES + n)
            def _():
                def down(c):
                    rows = pl.ds(pl.multiple_of(c * tm, tm), tm)
                    acc = bd_ref[0] + jnp.zeros((tm, FF_TILE), F32)
                    for f in range(N_FF_TILES):
                        w = wd_ref[0, f * FF_TILE:(f + 1) * FF_TILE, :].astype(BF16)
                        acc = acc + jnp.dot(h_scr[f, rows, :], w, preferred_element_type=F32)
                    y_scr[slot, rows, :] = acc
                    y_copy(n, slot, c).start()
                for_chunks(down)
                if n > 0:
                    for_chunks(lambda c: y_copy(n - 1, 1 - slot, c).wait())
                if n == N_FF_TILES - 1:
                    for_chunks(lambda c: y_copy(n, slot, c).wait())


def _experts(xs, tabs, layer, w_gate, b_gate, w_up, b_up, w_down, b_down):
    n_l, n_e, d, f = w_gate.shape
    assert f == N_FF_TILES * FF_TILE and d == N_FF_TILES * FF_TILE
    w_gate, w_up, w_down = (w.reshape(n_l * n_e, *w.shape[2:]) for w in (w_gate, w_up, w_down))
    b_gate, b_up, b_down = (b.reshape(n_l * n_e, 1, b.shape[2]) for b in (b_gate, b_up, b_down))
    m_total = xs.shape[0]
    sb_rows = SB_CHUNKS * ROW_TILE
    last = N_FF_TILES - 1

    def up_map(s, j, sbe, r0, nch, nu):
        return (sbe[s], 0, jnp.where(s < nu[0], jnp.minimum(j, last), last))

    def down_map(s, j, sbe, r0, nch, nu):
        return (sbe[s], 0, jnp.where(s < nu[0], jnp.maximum(j - N_FF_TILES, 0), last))

    return pl.pallas_call(
        _expert_kernel,
        grid_spec=pltpu.PrefetchScalarGridSpec(
            num_scalar_prefetch=4, grid=(tabs["s_max"], 2 * N_FF_TILES),
            in_specs=[pl.BlockSpec(memory_space=pl.ANY),
                      pl.BlockSpec((1, d, FF_TILE), up_map),
                      pl.BlockSpec((1, d, FF_TILE), up_map),
                      pl.BlockSpec((1, f, FF_TILE), down_map),
                      pl.BlockSpec((1, 1, FF_TILE), up_map),
                      pl.BlockSpec((1, 1, FF_TILE), up_map),
                      pl.BlockSpec((1, 1, FF_TILE), down_map)],
            out_specs=pl.BlockSpec(memory_space=pl.ANY),
            scratch_shapes=[pltpu.VMEM((sb_rows, d), F32),
                            pltpu.VMEM((N_FF_TILES, sb_rows, FF_TILE), BF16),
                            pltpu.VMEM((2, sb_rows, FF_TILE), F32),
                            pltpu.SemaphoreType.DMA((SB_CHUNKS,)),
                            pltpu.SemaphoreType.DMA((2,))]),
        out_shape=jax.ShapeDtypeStruct((m_total, d), F32),
        compiler_params=_cparams(("arbitrary", "arbitrary"), 56),
        name="moe_experts",
    )(tabs["sb_e"] + layer * n_e, tabs["sb_row0"], tabs["sb_nch"], tabs["n_used"], xs,
      w_gate, w_up, w_down, b_gate, b_up, b_down)


def _combine_kernel(dest_ref, ys_ref, h_ref, gate_ref, g_ref, b_ref, o_ref, buf, sem, *, alpha):
    i = pl.program_id(0)
    n_steps = pl.num_programs(0)
    tt = COMBINE_TOKENS

    def row_copy(step, slot, t, k):
        return pltpu.make_async_copy(
            ys_ref.at[pl.ds(dest_ref[(step * tt + t) * TOP_K + k], 1)],
            buf.at[slot, k, pl.ds(t, 1)], sem.at[slot])

    def issue(step, slot):
        def body(t, c):
            for k in range(TOP_K):
                row_copy(step, slot, t, k).start()
            return c
        lax.fori_loop(0, tt, body, 0)

    @pl.when(i == 0)
    def _():
        issue(0, 0)

    @pl.when(i + 1 < n_steps)
    def _():
        issue(i + 1, (i + 1) % 2)

    slot = i % 2

    pltpu.make_async_copy(buf.at[slot], buf.at[slot], sem.at[slot]).wait()

    gates = gate_ref[...]
    y = gates[:, 0:1] * buf[slot, 0]
    for k in range(1, TOP_K):
        y = y + gates[:, k:k + 1] * buf[slot, k]
    o_ref[...] = _layer_norm(alpha * h_ref[...] + y, g_ref[...], b_ref[...])


def _combine(ys, dest, h, gates, ln_g, ln_b, alpha):
    n, d = h.shape
    tt = COMBINE_TOKENS
    assert n % tt == 0
    return pl.pallas_call(
        functools.partial(_combine_kernel, alpha=alpha),
        grid_spec=pltpu.PrefetchScalarGridSpec(
            num_scalar_prefetch=1, grid=(n // tt,),
            in_specs=[pl.BlockSpec(memory_space=pl.ANY),
                      pl.BlockSpec((tt, d), lambda i, dr: (i, 0)),
                      pl.BlockSpec((tt, TOP_K), lambda i, dr: (i, 0)),
                      pl.BlockSpec((1, d), lambda i, dr: (0, 0)),
                      pl.BlockSpec((1, d), lambda i, dr: (0, 0))],
            out_specs=pl.BlockSpec((tt, d), lambda i, dr: (i, 0)),
            scratch_shapes=[pltpu.VMEM((2, TOP_K, tt, d), F32),
                            pltpu.SemaphoreType.DMA((2,))]),
        out_shape=jax.ShapeDtypeStruct((n, d), F32),
        compiler_params=_cparams(("arbitrary",), 40),
        name="moe_combine",
    )(dest, ys, h, gates, ln_g.reshape(1, d), ln_b.reshape(1, d))


def _moe_layer(h, idx, gates, ln_g, ln_b, layer, w_gate, b_gate, w_up, b_up, w_down, b_down, alpha):
    tabs = _routing_tables(idx, w_gate.shape[1])
    xs = _dispatch(h, tabs)
    ys = _experts(xs, tabs, layer, w_gate, b_gate, w_up, b_up, w_down, b_down)
    return _combine(ys, tabs["dest"], h, gates, ln_g, ln_b, alpha)


def _proj_kernel(x_ref, w_ref, lb_ref, o_ref, *, tiles_per_section):
    sec = pl.program_id(1) // tiles_per_section
    p = jnp.dot(x_ref[...], w_ref[...], preferred_element_type=F32)
    sig = jax.nn.sigmoid(p)
    lb = lb_ref[...]
    forget = lb + (1.0 - lb) * sig
    o_ref[...] = jnp.where(sec == 1, forget, jnp.where(sec == 2, p, p * sig))


def _hgrn_proj(h, w_in, lb):
    n, d = h.shape
    tn = 512
    tm = 1160
    assert n % tm == 0 and d % tn == 0 and w_in.shape[1] == 4 * d
    tps = d // tn
    return pl.pallas_call(
        functools.partial(_proj_kernel, tiles_per_section=tps),
        grid=(n // tm, 4 * tps),
        in_specs=[pl.BlockSpec((tm, d), lambda i, j: (i, 0)),
                  pl.BlockSpec((d, tn), lambda i, j: (0, j)),
                  pl.BlockSpec((1, tn), lambda i, j: (0, j % tps))],
        out_specs=pl.BlockSpec((tm, tn), lambda i, j: (i, j)),
        out_shape=jax.ShapeDtypeStruct((n, 4 * d), F32),
        compiler_params=_cparams(("arbitrary", "arbitrary"), 48),
        name="hgrn_proj",
    )(h, w_in, lb.reshape(1, d))


_NT = (((1,), (1,)), ((), ()))
_TN = (((0,), (0,)), ((), ()))


def _hgrn_block(q, f, v, st):
    c = q.shape[0]
    n_sub = c // SUB
    lf = jnp.log(f)
    k = 1.0 - f
    row = lax.broadcasted_iota(jnp.int32, (c, 1), 0)
    loc = row % SUB
    bl = lf
    for sh in (1, 2, 4, 8):
        bl = bl + jnp.where(loc >= sh, pltpu.roll(bl, sh, axis=0), 0.0)
    tot = [bl[SUB * i + SUB - 1:SUB * i + SUB, :] for i in range(n_sub)]
    off = [jnp.zeros_like(tot[0])]
    for i in range(1, n_sub):
        off.append(off[-1] + tot[i - 1])
    b_last = off[-1] + tot[-1]
    if n_sub > 1:
        blk = row // SUB
        offs = off[0]
        tots = tot[0]
        for i in range(1, n_sub):
            offs = jnp.where(blk == i, off[i], offs)
            tots = jnp.where(blk == i, tot[i], tots)
        b = bl + offs
    else:
        b = bl
        tots = tot[0]
    r2 = lax.broadcasted_iota(jnp.int32, (c, c), 0)
    c2 = lax.broadcasted_iota(jnp.int32, (c, c), 1)
    a_loc = lax.dot_general(q * jnp.exp(bl), k * jnp.exp(-bl), _NT, preferred_element_type=F32)
    a = jnp.where((r2 // SUB == c2 // SUB) & (c2 <= r2), a_loc, 0.0)
    if n_sub > 1:
        k_end = k * jnp.exp(tots - bl)
        for jb in range(n_sub - 1):
            gamma = off[jb] + tot[jb]
            q_rel = q * jnp.exp(jnp.minimum(b - gamma, 0.0))
            a_j = lax.dot_general(q_rel, k_end, _NT, preferred_element_type=F32)
            a = jnp.where((c2 // SUB == jb) & (r2 // SUB > jb), a_j, a)
    o = jnp.dot(a, v, preferred_element_type=F32)
    o = o + lax.dot_general(q * jnp.exp(b), st, _NT, preferred_element_type=F32)
    kd = k * jnp.exp(b_last - b)
    st_new = st * jnp.exp(b_last) + lax.dot_general(v, kd, _TN, preferred_element_type=F32)
    o = o * lax.rsqrt(jnp.mean(o * o, axis=-1, keepdims=True) + RMS_EPS)
    return o, st_new


def _rec_prompt_kernel(q_ref, f_ref, v_ref, o_ref, s_ref, st_scr, *, n_blocks):
    st_scr[...] = jnp.zeros_like(st_scr)

    def run(r0, c):
        rows = pl.ds(r0, c)
        for hd in range(REC_HEADS):
            cols = slice(hd * HG_DK, (hd + 1) * HG_DK)
            o, st = _hgrn_block(q_ref[rows, cols], f_ref[rows, cols], v_ref[rows, cols], st_scr[hd])
            o_ref[rows, cols] = o
            st_scr[hd] = st

    run(0, SUB)

    def body(i, carry):
        run(pl.multiple_of(SUB + i * REC_BLOCK, SUB), REC_BLOCK)
        return carry
    lax.fori_loop(0, n_blocks, body, 0)
    for hd in range(REC_HEADS):
        s_ref[0, hd] = st_scr[hd].T


def _rec_sample_kernel(q_ref, f_ref, v_ref, s0_ref, o_ref, s_ref, *, t_new):
    sb = s0_ref.shape[1]

    def body(i, carry):
        for u in range(REC_SEQS):
            sq = i * REC_SEQS + u
            rows = pl.ds(pl.multiple_of(sq * t_new, t_new), t_new)
            o, st = _hgrn_block_short(q_ref[rows, :], f_ref[rows, :], v_ref[rows, :],
                                      s0_ref[0, sq, 0].T)
            o_ref[rows, :] = o
            s_ref[0, sq, 0] = st.T
        return carry
    lax.fori_loop(0, sb // REC_SEQS, body, 0)


def _hgrn_block_short(q, f, v, st):
    c = q.shape[0]
    lf = jnp.log(f)
    k = 1.0 - f
    row = lax.broadcasted_iota(jnp.int32, (c, 1), 0)
    bl = lf
    sh = 1
    while sh < c:
        bl = bl + jnp.where(row >= sh, pltpu.roll(bl, sh, axis=0), 0.0)
        sh *= 2
    b_last = bl[c - 1:c, :]
    r2 = lax.broadcasted_iota(jnp.int32, (c, c), 0)
    c2 = lax.broadcasted_iota(jnp.int32, (c, c), 1)
    qe = q * jnp.exp(bl)
    a = lax.dot_general(qe, k * jnp.exp(-bl), _NT, preferred_element_type=F32)
    a = jnp.where(c2 <= r2, a, 0.0)
    o = jnp.dot(a, v, preferred_element_type=F32)
    o = o + lax.dot_general(qe, st, _NT, preferred_element_type=F32)
    kd = k * jnp.exp(b_last - bl)
    st_new = st * jnp.exp(b_last) + lax.dot_general(v, kd, _TN, preferred_element_type=F32)
    o = o * lax.rsqrt(jnp.mean(o * o, axis=-1, keepdims=True) + RMS_EPS)
    return o, st_new


def _hgrn_recurrence(proj, state_s, b_p, t_p, b_s, t_s):
    n, d4 = proj.shape
    d = d4 // 4
    heads = d // HG_DK
    n_p = b_p * t_p
    assert (t_p - SUB) % REC_BLOCK == 0 and t_s < SUB and t_s % 8 == 0
    assert heads % REC_HEADS == 0
    hgroups = heads // REC_HEADS
    gw = REC_HEADS * HG_DK
    o_p, s_p = pl.pallas_call(
        functools.partial(_rec_prompt_kernel, n_blocks=(t_p - SUB) // REC_BLOCK),
        grid=(b_p, hgroups),
        in_specs=[pl.BlockSpec((t_p, gw), lambda b, h: (b, h)),
                  pl.BlockSpec((t_p, gw), lambda b, h: (b, hgroups + h)),
                  pl.BlockSpec((t_p, gw), lambda b, h: (b, 2 * hgroups + h))],
        out_specs=[pl.BlockSpec((t_p, gw), lambda b, h: (b, h)),
                   pl.BlockSpec((1, REC_HEADS, HG_DK, HG_DK), lambda b, h: (b, h, 0, 0))],
        out_shape=[jax.ShapeDtypeStruct((n_p, d), F32),
                   jax.ShapeDtypeStruct((b_p, heads, HG_DK, HG_DK), F32)],
        scratch_shapes=[pltpu.VMEM((REC_HEADS, HG_DK, HG_DK), F32)],
        compiler_params=_cparams(("arbitrary", "arbitrary"), 48),
        name="hgrn_rec_prompt",
    )(proj, proj, proj)

    proj_s = proj[n_p:]
    sb = 32
    assert b_s % sb == 0
    o_s, s_s = pl.pallas_call(
        functools.partial(_rec_sample_kernel, t_new=t_s),
        grid=(heads, b_s // sb),
        in_specs=[pl.BlockSpec((sb * t_s, HG_DK), lambda h, i: (i, h)),
                  pl.BlockSpec((sb * t_s, HG_DK), lambda h, i: (i, heads + h)),
                  pl.BlockSpec((sb * t_s, HG_DK), lambda h, i: (i, 2 * heads + h)),
                  pl.BlockSpec((1, sb, 1, HG_DK, HG_DK), lambda h, i: (0, i, h, 0, 0))],
        out_specs=[pl.BlockSpec((sb * t_s, HG_DK), lambda h, i: (i, h)),
                   pl.BlockSpec((1, sb, 1, HG_DK, HG_DK), lambda h, i: (0, i, h, 0, 0))],
        out_shape=[jax.ShapeDtypeStruct((b_s * t_s, d), F32),
                   jax.ShapeDtypeStruct((1, b_s, heads, HG_DK, HG_DK), F32)],
        compiler_params=_cparams(("arbitrary", "arbitrary"), 32),
        name="hgrn_rec_sample",
    )(proj_s, proj_s, proj_s, state_s)
    return jnp.concatenate([o_p, o_s], axis=0), s_p, s_s


def _hgrn_out_kernel(o_ref, sg_ref, ng_ref, w_ref, x_ref, g_ref, b_ref, rw_ref, rb_ref,
                     h_ref, idx_ref, gate_ref, *, alpha):
    z = (o_ref[...] * ng_ref[...] * sg_ref[...]).astype(BF16)
    y = jnp.dot(z, w_ref[...], preferred_element_type=F32)
    _mix_post(x_ref[...], y, alpha, g_ref, b_ref, rw_ref, rb_ref, h_ref, idx_ref, gate_ref)


def _hgrn_out(o, proj, norm_g, w_out, x, ln_g, ln_b, router_w, router_b, alpha):
    n, d = x.shape
    n_e = router_w.shape[1]
    tm = 464
    assert n % tm == 0
    consts = (ln_g.reshape(1, d), ln_b.reshape(1, d), router_w, router_b.reshape(1, n_e))
    row = lambda i: (i, 0)
    return pl.pallas_call(
        functools.partial(_hgrn_out_kernel, alpha=alpha),
        grid=(n // tm,),
        in_specs=[pl.BlockSpec((tm, d), row),
                  pl.BlockSpec((tm, d), lambda i: (i, 3)),
                  _const_spec((1, d)),
                  _const_spec((d, d)),
                  pl.BlockSpec((tm, d), row)] + [_const_spec(c.shape) for c in consts],
        out_specs=[pl.BlockSpec((tm, d), row), pl.BlockSpec((tm, TOP_K), row),
                   pl.BlockSpec((tm, TOP_K), row)],
        out_shape=[jax.ShapeDtypeStruct((n, d), F32),
                   jax.ShapeDtypeStruct((n, TOP_K), jnp.int32),
                   jax.ShapeDtypeStruct((n, TOP_K), F32)],
        compiler_params=_cparams(("arbitrary",), 56),
        name="hgrn_out",
    )(o, proj, norm_g.reshape(1, d), w_out.astype(BF16), x, *consts)


def kernel(x_prompt, x_sample, state_pool, state_hgrn, meta_tokens, pool_w, pool_scale, hg_w_in, hg_lb, hg_norm_g, hg_w_out, ln_g, ln_b, router_w, router_b, w_gate, b_gate, w_up, b_up, w_down, b_down):
    depth = ln_g.shape[0]
    assert depth == 2, "layer 0 = pooling mixer, layer 1 = HGRN2 mixer"
    alpha = float((2 * depth) ** 0.25)
    b_p, seq, d = x_prompt.shape
    b_s, t_s, _ = x_sample.shape
    t_p = seq + N_META
    n_p = b_p * t_p

    xp = jnp.concatenate([jnp.broadcast_to(meta_tokens[None], (b_p, N_META, d)), x_prompt], axis=1)
    prefix = state_pool[0]
    xs_ext = jnp.concatenate([jnp.zeros((b_s, 16 - POOL_BUF, d), F32), prefix, x_sample], axis=1)

    h, idx, gates = _pool_layer(xp, xs_ext, pool_w[0], pool_scale[0], ln_g[0, 0], ln_b[0, 0],
                                router_w[0], router_b[0], alpha)
    moe_w = (w_gate, b_gate, w_up, b_up, w_down, b_down)
    h = _moe_layer(h, idx, gates, ln_g[0, 1], ln_b[0, 1], 0, *moe_w, alpha)

    sm = jax.nn.softmax(hg_lb.astype(F32), axis=0)
    lb = (jnp.cumsum(sm, axis=0) - sm[0:1])[1]
    proj = _hgrn_proj(h, hg_w_in[0], lb)
    o, s_p, s_s = _hgrn_recurrence(proj, state_hgrn, b_p, t_p, b_s, t_s)
    h, idx, gates = _hgrn_out(o, proj, hg_norm_g[0], hg_w_out[0], h, ln_g[1, 0], ln_b[1, 0],
                              router_w[1], router_b[1], alpha)
    h = _moe_layer(h, idx, gates, ln_g[1, 1], ln_b[1, 1], 1, *moe_w, alpha)

    y_prompt = h[:n_p].reshape(b_p, t_p, d)[:, N_META:]
    y_sample = h[n_p:].reshape(b_s, t_s, d)
    pool_p = x_prompt[:, seq - POOL_BUF:][None]
    pool_s = jnp.concatenate([prefix, x_sample], axis=1)[:, t_s:][None]
    return (y_prompt, y_sample, pool_p, pool_s, s_p[None], s_s)
```

```python
import functools

import jax
import jax.numpy as jnp
from jax import lax
from jax.experimental import pallas as pl
from jax.experimental.pallas import tpu as pltpu

N_META = 16
POOL_WINDOWS = (2, 4, 8, 16)
POOL_BUF = max(POOL_WINDOWS) - 1
PAST_LEN = 16384
HG_DK = 128
SUB = 16
TOP_K = 4
SWIGLU_LIMIT = 7.0
SWIGLU_ALPHA = 1.702
LN_EPS = 1e-5
RMS_EPS = 1e-6

PAD_TILE = 128
CHUNKS = (512, 256, 128)
SB_ROWS = 1536
FF_TILE = 512
N_FF_TILES = 4
REC_BLOCK = 64
REC_HEADS = 4
REC_SEQS = 8
DISPATCH_TILE = 640
COMBINE_TOKENS = 160
PROJ_SPLIT = 5
OUT_SPLIT = 2
MIB = 1024 * 1024

F32 = jnp.float32
BF16 = jnp.bfloat16


def _cparams(sem, vmem_mib):
    return pltpu.CompilerParams(dimension_semantics=sem, vmem_limit_bytes=vmem_mib * MIB)


def _layer_norm(x, g, b):
    mu = jnp.mean(x, axis=-1, keepdims=True)
    xc = x - mu
    var = jnp.mean(xc * xc, axis=-1, keepdims=True)
    return xc * lax.rsqrt(var + LN_EPS) * g + b


def _dot_3pass(a, b):
    a_hi = a.astype(BF16)
    a_lo = (a - a_hi.astype(F32)).astype(BF16)
    b_hi = b.astype(BF16)
    b_lo = (b - b_hi.astype(F32)).astype(BF16)

    def d(x, y):
        return jnp.dot(x, y, preferred_element_type=F32)
    return d(a_hi, b_hi) + (d(a_hi, b_lo) + d(a_lo, b_hi))


def _route_rows(h, rw, rb):
    logits = _dot_3pass(h, rw) + rb
    n_e = logits.shape[-1]
    lane = lax.broadcasted_iota(jnp.int32, logits.shape, 1)
    vals, ids = [], []
    l = logits
    for _ in range(TOP_K):
        m = jnp.max(l, axis=-1, keepdims=True)
        ix = jnp.min(jnp.where(l == m, lane, n_e), axis=-1, keepdims=True)
        vals.append(m)
        ids.append(ix)
        l = jnp.where(lane == ix, -jnp.inf, l)
    v = jnp.concatenate(vals, axis=1)
    e = jnp.exp(v - vals[0])
    gates = e / jnp.sum(e, axis=-1, keepdims=True)
    return jnp.concatenate(ids, axis=1), gates


def _mix_post(x, y, alpha, g_ref, b_ref, rw_ref, rb_ref, h_ref, idx_ref, gate_ref, rows=slice(None)):
    h = _layer_norm(alpha * x + y, g_ref[...], b_ref[...])
    h_ref[rows, :] = h
    ids, gates = _route_rows(h, rw_ref[...], rb_ref[...])
    idx_ref[rows, :] = ids
    gate_ref[rows, :] = gates


def _window_sum(e, w):
    s = e
    sh = 1
    while sh < w:
        s = s + pltpu.roll(s, sh, axis=0)
        sh *= 2
    return s


def _pool_prompt_kernel(x_ref, halo_ref, pw_ref, ps_ref, g_ref, b_ref, rw_ref, rb_ref,
                        h_ref, idx_ref, gate_ref, *, tt, alpha):
    i = pl.program_id(1)
    cur = x_ref[0]
    halo = jnp.where(i > 0, halo_ref[0], 0.0)
    ext = jnp.concatenate([halo, cur], axis=0)
    grp = cur.shape[1] // len(POOL_WINDOWS)
    pos = i * tt + lax.broadcasted_iota(jnp.int32, (tt, 1), 0)
    ys = []
    for gi, w in enumerate(POOL_WINDOWS):
        s = _window_sum(ext[:, gi * grp:(gi + 1) * grp], w)[16:]
        cnt = jnp.minimum(w, pos + 1).astype(F32)
        d = s / cnt - cur[:, gi * grp:(gi + 1) * grp]
        ys.append(_dot_3pass(d, pw_ref[gi]))
    y = jnp.concatenate(ys, axis=1) * ps_ref[...]
    _mix_post(cur, y, alpha, g_ref, b_ref, rw_ref, rb_ref, h_ref, idx_ref, gate_ref)


def _pool_sample_kernel(x_ref, pw_ref, ps_ref, g_ref, b_ref, rw_ref, rb_ref,
                        h_ref, idx_ref, gate_ref, *, alpha, t_new):
    sb, te, d_model = x_ref.shape
    ext = x_ref[...].reshape(sb * te, d_model)
    grp = d_model // len(POOL_WINDOWS)
    cur = x_ref[:, 16:, :].reshape(sb * t_new, d_model)
    tpos = lax.broadcasted_iota(jnp.int32, (sb, t_new, 1), 1).reshape(sb * t_new, 1)
    ys = []
    for gi, w in enumerate(POOL_WINDOWS):
        s = _window_sum(ext[:, gi * grp:(gi + 1) * grp], w)
        s = s.reshape(sb, te, grp)[:, 16:, :].reshape(sb * t_new, grp)
        cnt = jnp.minimum(w, PAST_LEN + tpos + 1).astype(F32)
        d = s / cnt - cur[:, gi * grp:(gi + 1) * grp]
        ys.append(_dot_3pass(d, pw_ref[gi]))
    y = jnp.concatenate(ys, axis=1) * ps_ref[...]
    _mix_post(cur, y, alpha, g_ref, b_ref, rw_ref, rb_ref, h_ref, idx_ref, gate_ref)


def _const_spec(shape):
    nd = len(shape)
    return pl.BlockSpec(shape, lambda *_: (0,) * nd)


def _pool_layer(xp, xs_ext, pool_w, pool_scale, ln_g, ln_b, router_w, router_b, alpha):
    b_p, t_p, d = xp.shape
    n_e = router_w.shape[1]
    tt = 688
    assert t_p % tt == 0 and tt % 16 == 0
    ps = pool_scale.reshape(1, d)
    g2, b2, rb2 = ln_g.reshape(1, d), ln_b.reshape(1, d), router_b.reshape(1, n_e)
    consts = (pool_w, ps, g2, b2, router_w, rb2)
    const_specs = [_const_spec(c.shape) for c in consts]
    n_p = b_p * t_p
    nt = t_p // tt
    hp, ip, gp = pl.pallas_call(
        functools.partial(_pool_prompt_kernel, tt=tt, alpha=alpha),
        grid=(b_p, nt),
        in_specs=[pl.BlockSpec((1, tt, d), lambda b, i: (b, i, 0)),
                  pl.BlockSpec((1, 16, d), lambda b, i: (b, jnp.maximum(i * (tt // 16) - 1, 0), 0))]
        + const_specs,
        out_specs=[pl.BlockSpec((tt, d), lambda b, i: (b * nt + i, 0)),
                   pl.BlockSpec((tt, TOP_K), lambda b, i: (b * nt + i, 0)),
                   pl.BlockSpec((tt, TOP_K), lambda b, i: (b * nt + i, 0))],
        out_shape=[jax.ShapeDtypeStruct((n_p, d), F32),
                   jax.ShapeDtypeStruct((n_p, TOP_K), jnp.int32),
                   jax.ShapeDtypeStruct((n_p, TOP_K), F32)],
        compiler_params=_cparams(("arbitrary", "arbitrary"), 56),
        name="pool_prompt",
    )(xp, xp, *consts)

    b_s, te, _ = xs_ext.shape
    t_new = te - 16
    sb = 16
    assert b_s % sb == 0
    n_s = b_s * t_new
    hs, is_, gs = pl.pallas_call(
        functools.partial(_pool_sample_kernel, alpha=alpha, t_new=t_new),
        grid=(b_s // sb,),
        in_specs=[pl.BlockSpec((sb, te, d), lambda i: (i, 0, 0))] + const_specs,
        out_specs=[pl.BlockSpec((sb * t_new, d), lambda i: (i, 0)),
                   pl.BlockSpec((sb * t_new, TOP_K), lambda i: (i, 0)),
                   pl.BlockSpec((sb * t_new, TOP_K), lambda i: (i, 0))],
        out_shape=[jax.ShapeDtypeStruct((n_s, d), F32),
                   jax.ShapeDtypeStruct((n_s, TOP_K), jnp.int32),
                   jax.ShapeDtypeStruct((n_s, TOP_K), F32)],
        compiler_params=_cparams(("arbitrary",), 40),
        name="pool_sample",
    )(xs_ext, *consts)
    return (jnp.concatenate([hp, hs], axis=0), jnp.concatenate([ip, is_], axis=0),
            jnp.concatenate([gp, gs], axis=0))


def _routing_tables(idx, n_experts):
    n, k = idx.shape
    a = n * k
    e_flat = idx.reshape(a)
    experts = jnp.arange(n_experts, dtype=jnp.int32)
    onehot = (e_flat[:, None] == experts[None, :]).astype(jnp.int32)
    csum = jnp.cumsum(onehot, axis=0)
    counts = csum[-1]
    pcounts = ((counts + PAD_TILE - 1) // PAD_TILE) * PAD_TILE
    pend = jnp.cumsum(pcounts)
    poff = pend - pcounts
    dest = jnp.sum(onehot * (csum - onehot + poff[None, :]), axis=1)

    m_rows = -(-(a + n_experts * (PAD_TILE - 1)) // PAD_TILE) * PAD_TILE
    n_pad = n_experts * (PAD_TILE - 1)
    n_pad_entries = -(-n_pad // DISPATCH_TILE) * DISPATCH_TILE
    j = jnp.arange(PAD_TILE - 1, dtype=jnp.int32)[None, :]
    trash = m_rows + jnp.arange(n_pad_entries, dtype=jnp.int32)
    pad_dest = jnp.where(j < (pcounts - counts)[:, None], (poff + counts)[:, None] + j,
                         trash[:n_pad].reshape(n_experts, PAD_TILE - 1)).reshape(n_pad)
    dest_all = jnp.concatenate([dest, pad_dest, trash[n_pad:]]).astype(jnp.int32)
    m_total = m_rows + n_pad_entries

    sb_rows = SB_ROWS
    s_max = n_experts + m_rows // sb_rows
    nsb = (pcounts + sb_rows - 1) // sb_rows
    sb_end = jnp.cumsum(nsb)
    n_used = sb_end[-1]
    s = jnp.arange(s_max, dtype=jnp.int32)
    s_eff = jnp.minimum(s, n_used - 1)
    e_of_s = jnp.sum((s_eff[:, None] >= sb_end[None, :]).astype(jnp.int32), axis=1)
    e_of_s = jnp.minimum(e_of_s, n_experts - 1)
    oh_s = (e_of_s[:, None] == experts[None, :]).astype(jnp.int32)
    local = s_eff - jnp.sum(oh_s * (sb_end - nsb)[None, :], axis=1)
    row0 = jnp.sum(oh_s * poff[None, :], axis=1) + local * sb_rows
    rows = jnp.clip(jnp.sum(oh_s * pcounts[None, :], axis=1) - local * sb_rows, 0, sb_rows)
    rows = jnp.where(s < n_used, rows, 0)
    return dict(dest=dest.astype(jnp.int32), dest_all=dest_all, m_total=m_total,
                sb_e=e_of_s.astype(jnp.int32), sb_row0=row0.astype(jnp.int32),
                sb_rows=rows.astype(jnp.int32), n_used=n_used.reshape(1).astype(jnp.int32),
                s_max=s_max, n_main=a)


def _dispatch_kernel(dest_ref, h_ref, xs_ref, z_scr, sem, *, n_main_steps):
    i = pl.program_id(0)
    base = i * DISPATCH_TILE

    def row_copy(src_ref, src_row, a):
        return pltpu.make_async_copy(src_ref.at[pl.ds(src_row, 1)],
                                     xs_ref.at[pl.ds(dest_ref[base + a], 1)], sem)

    @pl.when(i == 0)
    def _():
        z_scr[...] = jnp.zeros_like(z_scr)

    @pl.when(i < n_main_steps)
    def _():
        def issue(t, c):
            for k in range(TOP_K):
                row_copy(h_ref, t, t * TOP_K + k).start()
            return c
        lax.fori_loop(0, DISPATCH_TILE // TOP_K, issue, 0)

    @pl.when(i >= n_main_steps)
    def _():
        def issue(a, c):
            row_copy(z_scr, 0, a).start()
            return c
        lax.fori_loop(0, DISPATCH_TILE, issue, 0)

    all_rows = xs_ref.at[pl.ds(0, DISPATCH_TILE)]
    pltpu.make_async_copy(all_rows, all_rows, sem).wait()


def _dispatch(h, tabs):
    n, d = h.shape
    n_main = tabs["n_main"]
    assert n_main % DISPATCH_TILE == 0
    n_steps = tabs["dest_all"].shape[0] // DISPATCH_TILE
    n_main_steps = n_main // DISPATCH_TILE
    tt = DISPATCH_TILE // TOP_K
    return pl.pallas_call(
        functools.partial(_dispatch_kernel, n_main_steps=n_main_steps),
        grid_spec=pltpu.PrefetchScalarGridSpec(
            num_scalar_prefetch=1, grid=(n_steps,),
            in_specs=[pl.BlockSpec((tt, d), lambda i, dr: (jnp.minimum(i, n_main_steps - 1), 0))],
            out_specs=pl.BlockSpec(memory_space=pl.ANY),
            scratch_shapes=[pltpu.VMEM((8, d), F32), pltpu.SemaphoreType.DMA(())]),
        out_shape=jax.ShapeDtypeStruct((tabs["m_total"], d), F32),
        compiler_params=_cparams(("arbitrary",), 16),
        name="moe_dispatch",
    )(tabs["dest_all"], h)


def _expert_kernel(sbe_ref, row0_ref, rows_ref, nused_ref, xs_ref, wg_ref, wu_ref, wd_ref,
                   bg_ref, bu_ref, bd_ref, ys_ref, x_scr, h_scr, y_scr, x_sem, y_sem):
    del sbe_ref, nused_ref
    s = pl.program_id(0)
    j = pl.program_id(1)
    rows = rows_ref[s]
    row0 = pl.multiple_of(row0_ref[s], PAD_TILE)
    big = CHUNKS[0]
    n_big = rows // big

    def for_chunks(fn):
        def body(c, carry):
            fn(pl.multiple_of(c * big, big), big)
            return carry
        lax.fori_loop(0, n_big, body, 0)
        off = n_big * big
        rem = rows - off
        for tm in CHUNKS[1:]:
            has = (rem & tm) != 0

            @pl.when(has)
            def _(off=off, tm=tm):
                fn(pl.multiple_of(off, tm), tm)
            off = off + jnp.where(has, tm, 0)

    def x_piece(p):
        r = pl.multiple_of(p * PAD_TILE, PAD_TILE)
        return pltpu.make_async_copy(xs_ref.at[pl.ds(pl.multiple_of(row0 + r, PAD_TILE), PAD_TILE)],
                                     x_scr.at[pl.ds(r, PAD_TILE)], x_sem.at[p])

    def y_copy(n, slot, r0, tm):
        return pltpu.make_async_copy(
            y_scr.at[slot, pl.ds(r0, tm)],
            ys_ref.at[pl.ds(pl.multiple_of(row0 + r0, PAD_TILE), tm), pl.ds(n * FF_TILE, FF_TILE)],
            y_sem.at[slot])

    @pl.when(rows > 0)
    def _():
        @pl.when(j == 0)
        def _():
            def start(p, carry):
                x_piece(p).start()
                return carry
            lax.fori_loop(0, rows // PAD_TILE, start, 0)

        @pl.when(j < N_FF_TILES)
        def _():
            def gate_up(r0, tm):
                @pl.when(j == 0)
                def _():
                    for q in range(tm // PAD_TILE):
                        x_piece(r0 // PAD_TILE + q).wait()
                x = x_scr[pl.ds(r0, tm), :]
                hg = jnp.dot(x, wg_ref[0], preferred_element_type=F32) + bg_ref[0]
                hu = jnp.dot(x, wu_ref[0], preferred_element_type=F32) + bu_ref[0]
                hg = jnp.minimum(hg, SWIGLU_LIMIT)
                hu = jnp.clip(hu, -SWIGLU_LIMIT, SWIGLU_LIMIT)
                act = (hu + 1.0) * hg * jax.nn.sigmoid(SWIGLU_ALPHA * hg)
                h_scr[j, pl.ds(r0, tm), :] = act.astype(BF16)
            for_chunks(gate_up)

        @pl.when(j >= N_FF_TILES)
        def _():
            slot = (j - N_FF_TILES) % 2

            def down(r0, tm):
                acc = bd_ref[0] + jnp.zeros((tm, FF_TILE), F32)
                for f in range(N_FF_TILES):
                    w = wd_ref[0, f * FF_TILE:(f + 1) * FF_TILE, :].astype(BF16)
                    acc = acc + jnp.dot(h_scr[f, pl.ds(r0, tm), :], w, preferred_element_type=F32)
                y_scr[slot, pl.ds(r0, tm), :] = acc
                for n in range(N_FF_TILES):
                    @pl.when(j == N_FF_TILES + n)
                    def _(n=n):
                        y_copy(n, n % 2, r0, tm).start()
            for_chunks(down)

            @pl.when(j > N_FF_TILES)
            def _():
                for_chunks(lambda r0, tm: y_copy(0, 1 - slot, r0, tm).wait())

            @pl.when(j == 2 * N_FF_TILES - 1)
            def _():
                for_chunks(lambda r0, tm: y_copy(0, slot, r0, tm).wait())


def _experts(xs, tabs, layer, w_gate, b_gate, w_up, b_up, w_down, b_down):
    n_l, n_e, d, f = w_gate.shape
    assert f == N_FF_TILES * FF_TILE and d == N_FF_TILES * FF_TILE
    w_gate, w_up, w_down = (w.reshape(n_l * n_e, *w.shape[2:]) for w in (w_gate, w_up, w_down))
    b_gate, b_up, b_down = (b.reshape(n_l * n_e, 1, b.shape[2]) for b in (b_gate, b_up, b_down))
    m_total = xs.shape[0]
    sb_rows = SB_ROWS
    last = N_FF_TILES - 1

    def up_map(s, j, sbe, r0, nrows, nu):
        return (sbe[s], 0, jnp.where(s < nu[0], jnp.minimum(j, last), last))

    def down_map(s, j, sbe, r0, nrows, nu):
        return (sbe[s], 0, jnp.where(s < nu[0], jnp.maximum(j - N_FF_TILES, 0), last))

    return pl.pallas_call(
        _expert_kernel,
        grid_spec=pltpu.PrefetchScalarGridSpec(
            num_scalar_prefetch=4, grid=(tabs["s_max"], 2 * N_FF_TILES),
            in_specs=[pl.BlockSpec(memory_space=pl.ANY),
                      pl.BlockSpec((1, d, FF_TILE), up_map),
                      pl.BlockSpec((1, d, FF_TILE), up_map),
                      pl.BlockSpec((1, f, FF_TILE), down_map),
                      pl.BlockSpec((1, 1, FF_TILE), up_map),
                      pl.BlockSpec((1, 1, FF_TILE), up_map),
                      pl.BlockSpec((1, 1, FF_TILE), down_map)],
            out_specs=pl.BlockSpec(memory_space=pl.ANY),
            scratch_shapes=[pltpu.VMEM((sb_rows, d), F32),
                            pltpu.VMEM((N_FF_TILES, sb_rows, FF_TILE), BF16),
                            pltpu.VMEM((2, sb_rows, FF_TILE), F32),
                            pltpu.SemaphoreType.DMA((SB_ROWS // PAD_TILE,)),
                            pltpu.SemaphoreType.DMA((2,))]),
        out_shape=jax.ShapeDtypeStruct((m_total, d), F32),
        compiler_params=_cparams(("arbitrary", "arbitrary"), 56),
        name="moe_experts",
    )(tabs["sb_e"] + layer * n_e, tabs["sb_row0"], tabs["sb_rows"], tabs["n_used"], xs,
      w_gate, w_up, w_down, b_gate, b_up, b_down)


def _combine_kernel(dest_ref, ys_ref, h_ref, gate_ref, g_ref, b_ref, o_ref, buf, sem, *, alpha):
    i = pl.program_id(0)
    n_steps = pl.num_programs(0)
    tt = COMBINE_TOKENS

    def row_copy(step, slot, t, k):
        return pltpu.make_async_copy(
            ys_ref.at[pl.ds(dest_ref[(step * tt + t) * TOP_K + k], 1)],
            buf.at[slot, k, pl.ds(t, 1)], sem.at[slot])

    def issue(step, slot):
        def body(t, c):
            for k in range(TOP_K):
                row_copy(step, slot, t, k).start()
            return c
        lax.fori_loop(0, tt, body, 0)

    @pl.when(i == 0)
    def _():
        issue(0, 0)

    @pl.when(i + 1 < n_steps)
    def _():
        issue(i + 1, (i + 1) % 2)

    slot = i % 2

    pltpu.make_async_copy(buf.at[slot], buf.at[slot], sem.at[slot]).wait()

    gates = gate_ref[...]
    y = gates[:, 0:1] * buf[slot, 0]
    for k in range(1, TOP_K):
        y = y + gates[:, k:k + 1] * buf[slot, k]
    o_ref[...] = _layer_norm(alpha * h_ref[...] + y, g_ref[...], b_ref[...])


def _combine(ys, dest, h, gates, ln_g, ln_b, alpha):
    n, d = h.shape
    tt = COMBINE_TOKENS
    assert n % tt == 0
    return pl.pallas_call(
        functools.partial(_combine_kernel, alpha=alpha),
        grid_spec=pltpu.PrefetchScalarGridSpec(
            num_scalar_prefetch=1, grid=(n // tt,),
            in_specs=[pl.BlockSpec(memory_space=pl.ANY),
                      pl.BlockSpec((tt, d), lambda i, dr: (i, 0)),
                      pl.BlockSpec((tt, TOP_K), lambda i, dr: (i, 0)),
                      pl.BlockSpec((1, d), lambda i, dr: (0, 0)),
                      pl.BlockSpec((1, d), lambda i, dr: (0, 0))],
            out_specs=pl.BlockSpec((tt, d), lambda i, dr: (i, 0)),
            scratch_shapes=[pltpu.VMEM((2, TOP_K, tt, d), F32),
                            pltpu.SemaphoreType.DMA((2,))]),
        out_shape=jax.ShapeDtypeStruct((n, d), F32),
        compiler_params=_cparams(("arbitrary",), 40),
        name="moe_combine",
    )(dest, ys, h, gates, ln_g.reshape(1, d), ln_b.reshape(1, d))


def _moe_layer(h, idx, gates, ln_g, ln_b, layer, w_gate, b_gate, w_up, b_up, w_down, b_down, alpha):
    tabs = _routing_tables(idx, w_gate.shape[1])
    xs = _dispatch(h, tabs)
    ys = _experts(xs, tabs, layer, w_gate, b_gate, w_up, b_up, w_down, b_down)
    return _combine(ys, tabs["dest"], h, gates, ln_g, ln_b, alpha)


def _proj_kernel(x_ref, w_ref, lb_ref, o_ref, *, tiles_per_section):
    sec = pl.program_id(1) // tiles_per_section
    lb = lb_ref[...]
    sub = x_ref.shape[0] // PROJ_SPLIT
    for c in range(PROJ_SPLIT):
        rows = slice(c * sub, (c + 1) * sub)
        p = jnp.dot(x_ref[rows, :], w_ref[...], preferred_element_type=F32)
        sig = jax.nn.sigmoid(p)
        forget = lb + (1.0 - lb) * sig
        o_ref[rows, :] = jnp.where(sec == 1, forget, jnp.where(sec == 2, p, p * sig))


def _hgrn_proj(h, w_in, lb):
    n, d = h.shape
    tn = 512
    tm = 1160
    assert n % tm == 0 and d % tn == 0 and w_in.shape[1] == 4 * d and tm % (8 * PROJ_SPLIT) == 0
    tps = d // tn
    return pl.pallas_call(
        functools.partial(_proj_kernel, tiles_per_section=tps),
        grid=(n // tm, 4 * tps),
        in_specs=[pl.BlockSpec((tm, d), lambda i, j: (i, 0)),
                  pl.BlockSpec((d, tn), lambda i, j: (0, j)),
                  pl.BlockSpec((1, tn), lambda i, j: (0, j % tps))],
        out_specs=pl.BlockSpec((tm, tn), lambda i, j: (i, j)),
        out_shape=jax.ShapeDtypeStruct((n, 4 * d), F32),
        compiler_params=_cparams(("arbitrary", "arbitrary"), 48),
        name="hgrn_proj",
    )(h, w_in, lb.reshape(1, d))


_NT = (((1,), (1,)), ((), ()))
_TN = (((0,), (0,)), ((), ()))


def _hgrn_block(q, f, v, st):
    c = q.shape[0]
    n_sub = c // SUB
    lf = jnp.log(f)
    k = 1.0 - f
    row = lax.broadcasted_iota(jnp.int32, (c, 1), 0)
    loc = row % SUB
    bl = lf
    for sh in (1, 2, 4, 8):
        bl = bl + jnp.where(loc >= sh, pltpu.roll(bl, sh, axis=0), 0.0)
    tot = [bl[SUB * i + SUB - 1:SUB * i + SUB, :] for i in range(n_sub)]
    off = [jnp.zeros_like(tot[0])]
    for i in range(1, n_sub):
        off.append(off[-1] + tot[i - 1])
    b_last = off[-1] + tot[-1]
    if n_sub > 1:
        blk = row // SUB
        offs = off[0]
        tots = tot[0]
        for i in range(1, n_sub):
            offs = jnp.where(blk == i, off[i], offs)
            tots = jnp.where(blk == i, tot[i], tots)
        b = bl + offs
    else:
        b = bl
        tots = tot[0]
    r2 = lax.broadcasted_iota(jnp.int32, (c, c), 0)
    c2 = lax.broadcasted_iota(jnp.int32, (c, c), 1)
    a_loc = lax.dot_general(q * jnp.exp(bl), k * jnp.exp(-bl), _NT, preferred_element_type=F32)
    a = jnp.where((r2 // SUB == c2 // SUB) & (c2 <= r2), a_loc, 0.0)
    if n_sub > 1:
        k_end = k * jnp.exp(tots - bl)
        for jb in range(n_sub - 1):
            gamma = off[jb] + tot[jb]
            q_rel = q * jnp.exp(jnp.minimum(b - gamma, 0.0))
            a_j = lax.dot_general(q_rel, k_end, _NT, preferred_element_type=F32)
            a = jnp.where((c2 // SUB == jb) & (r2 // SUB > jb), a_j, a)
    o = jnp.dot(a, v, preferred_element_type=F32)
    o = o + lax.dot_general(q * jnp.exp(b), st, _NT, preferred_element_type=F32)
    kd = k * jnp.exp(b_last - b)
    st_new = st * jnp.exp(b_last) + lax.dot_general(v, kd, _TN, preferred_element_type=F32)
    o = o * lax.rsqrt(jnp.mean(o * o, axis=-1, keepdims=True) + RMS_EPS)
    return o, st_new


def _rec_prompt_kernel(q_ref, f_ref, v_ref, o_ref, s_ref, st_scr, *, n_blocks):
    st_scr[...] = jnp.zeros_like(st_scr)

    def run(r0, c):
        rows = pl.ds(r0, c)
        for hd in range(REC_HEADS):
            cols = slice(hd * HG_DK, (hd + 1) * HG_DK)
            o, st = _hgrn_block(q_ref[rows, cols], f_ref[rows, cols], v_ref[rows, cols], st_scr[hd])
            o_ref[rows, cols] = o
            st_scr[hd] = st

    run(0, SUB)

    def body(i, carry):
        run(pl.multiple_of(SUB + i * REC_BLOCK, SUB), REC_BLOCK)
        return carry
    lax.fori_loop(0, n_blocks, body, 0)
    for hd in range(REC_HEADS):
        s_ref[0, hd] = st_scr[hd].T


def _rec_sample_kernel(q_ref, f_ref, v_ref, s0_ref, o_ref, s_ref, *, t_new):
    sb = s0_ref.shape[1]

    def body(i, carry):
        for u in range(REC_SEQS):
            sq = i * REC_SEQS + u
            rows = pl.ds(pl.multiple_of(sq * t_new, t_new), t_new)
            o, st = _hgrn_block_short(q_ref[rows, :], f_ref[rows, :], v_ref[rows, :],
                                      s0_ref[0, sq, 0].T)
            o_ref[rows, :] = o
            s_ref[0, sq, 0] = st.T
        return carry
    lax.fori_loop(0, sb // REC_SEQS, body, 0)


def _hgrn_block_short(q, f, v, st):
    c = q.shape[0]
    lf = jnp.log(f)
    k = 1.0 - f
    row = lax.broadcasted_iota(jnp.int32, (c, 1), 0)
    bl = lf
    sh = 1
    while sh < c:
        bl = bl + jnp.where(row >= sh, pltpu.roll(bl, sh, axis=0), 0.0)
        sh *= 2
    b_last = bl[c - 1:c, :]
    r2 = lax.broadcasted_iota(jnp.int32, (c, c), 0)
    c2 = lax.broadcasted_iota(jnp.int32, (c, c), 1)
    qe = q * jnp.exp(bl)
    a = lax.dot_general(qe, k * jnp.exp(-bl), _NT, preferred_element_type=F32)
    a = jnp.where(c2 <= r2, a, 0.0)
    o = jnp.dot(a, v, preferred_element_type=F32)
    o = o + lax.dot_general(qe, st, _NT, preferred_element_type=F32)
    kd = k * jnp.exp(b_last - bl)
    st_new = st * jnp.exp(b_last) + lax.dot_general(v, kd, _TN, preferred_element_type=F32)
    o = o * lax.rsqrt(jnp.mean(o * o, axis=-1, keepdims=True) + RMS_EPS)
    return o, st_new


def _hgrn_recurrence(proj, state_s, b_p, t_p, b_s, t_s):
    n, d4 = proj.shape
    d = d4 // 4
    heads = d // HG_DK
    n_p = b_p * t_p
    assert (t_p - SUB) % REC_BLOCK == 0 and t_s < SUB and t_s % 8 == 0
    assert heads % REC_HEADS == 0
    hgroups = heads // REC_HEADS
    gw = REC_HEADS * HG_DK
    o_p, s_p = pl.pallas_call(
        functools.partial(_rec_prompt_kernel, n_blocks=(t_p - SUB) // REC_BLOCK),
        grid=(b_p, hgroups),
        in_specs=[pl.BlockSpec((t_p, gw), lambda b, h: (b, h)),
                  pl.BlockSpec((t_p, gw), lambda b, h: (b, hgroups + h)),
                  pl.BlockSpec((t_p, gw), lambda b, h: (b, 2 * hgroups + h))],
        out_specs=[pl.BlockSpec((t_p, gw), lambda b, h: (b, h)),
                   pl.BlockSpec((1, REC_HEADS, HG_DK, HG_DK), lambda b, h: (b, h, 0, 0))],
        out_shape=[jax.ShapeDtypeStruct((n_p, d), F32),
                   jax.ShapeDtypeStruct((b_p, heads, HG_DK, HG_DK), F32)],
        scratch_shapes=[pltpu.VMEM((REC_HEADS, HG_DK, HG_DK), F32)],
        compiler_params=_cparams(("arbitrary", "arbitrary"), 48),
        name="hgrn_rec_prompt",
    )(proj, proj, proj)

    proj_s = proj[n_p:]
    sb = 32
    assert b_s % sb == 0
    o_s, s_s = pl.pallas_call(
        functools.partial(_rec_sample_kernel, t_new=t_s),
        grid=(heads, b_s // sb),
        in_specs=[pl.BlockSpec((sb * t_s, HG_DK), lambda h, i: (i, h)),
                  pl.BlockSpec((sb * t_s, HG_DK), lambda h, i: (i, heads + h)),
                  pl.BlockSpec((sb * t_s, HG_DK), lambda h, i: (i, 2 * heads + h)),
                  pl.BlockSpec((1, sb, 1, HG_DK, HG_DK), lambda h, i: (0, i, h, 0, 0))],
        out_specs=[pl.BlockSpec((sb * t_s, HG_DK), lambda h, i: (i, h)),
                   pl.BlockSpec((1, sb, 1, HG_DK, HG_DK), lambda h, i: (0, i, h, 0, 0))],
        out_shape=[jax.ShapeDtypeStruct((b_s * t_s, d), F32),
                   jax.ShapeDtypeStruct((1, b_s, heads, HG_DK, HG_DK), F32)],
        compiler_params=_cparams(("arbitrary", "arbitrary"), 32),
        name="hgrn_rec_sample",
    )(proj_s, proj_s, proj_s, state_s)
    return jnp.concatenate([o_p, o_s], axis=0), s_p, s_s


def _hgrn_out_kernel(o_ref, sg_ref, ng_ref, w_ref, x_ref, g_ref, b_ref, rw_ref, rb_ref,
                     h_ref, idx_ref, gate_ref, *, alpha):
    sub = o_ref.shape[0] // OUT_SPLIT
    for c in range(OUT_SPLIT):
        rows = slice(c * sub, (c + 1) * sub)
        z = (o_ref[rows, :] * ng_ref[...] * sg_ref[rows, :]).astype(BF16)
        y = jnp.dot(z, w_ref[...], preferred_element_type=F32)
        _mix_post(x_ref[rows, :], y, alpha, g_ref, b_ref, rw_ref, rb_ref, h_ref, idx_ref, gate_ref, rows)


def _hgrn_out(o, proj, norm_g, w_out, x, ln_g, ln_b, router_w, router_b, alpha):
    n, d = x.shape
    n_e = router_w.shape[1]
    tm = 464
    assert n % tm == 0 and tm % (8 * OUT_SPLIT) == 0
    consts = (ln_g.reshape(1, d), ln_b.reshape(1, d), router_w, router_b.reshape(1, n_e))
    row = lambda i: (i, 0)
    return pl.pallas_call(
        functools.partial(_hgrn_out_kernel, alpha=alpha),
        grid=(n // tm,),
        in_specs=[pl.BlockSpec((tm, d), row),
                  pl.BlockSpec((tm, d), lambda i: (i, 3)),
                  _const_spec((1, d)),
                  _const_spec((d, d)),
                  pl.BlockSpec((tm, d), row)] + [_const_spec(c.shape) for c in consts],
        out_specs=[pl.BlockSpec((tm, d), row), pl.BlockSpec((tm, TOP_K), row),
                   pl.BlockSpec((tm, TOP_K), row)],
        out_shape=[jax.ShapeDtypeStruct((n, d), F32),
                   jax.ShapeDtypeStruct((n, TOP_K), jnp.int32),
                   jax.ShapeDtypeStruct((n, TOP_K), F32)],
        compiler_params=_cparams(("arbitrary",), 56),
        name="hgrn_out",
    )(o, proj, norm_g.reshape(1, d), w_out.astype(BF16), x, *consts)


def kernel(x_prompt, x_sample, state_pool, state_hgrn, meta_tokens, pool_w, pool_scale, hg_w_in, hg_lb, hg_norm_g, hg_w_out, ln_g, ln_b, router_w, router_b, w_gate, b_gate, w_up, b_up, w_down, b_down):
    depth = ln_g.shape[0]
    assert depth == 2, "layer 0 = pooling mixer, layer 1 = HGRN2 mixer"
    alpha = float((2 * depth) ** 0.25)
    b_p, seq, d = x_prompt.shape
    b_s, t_s, _ = x_sample.shape
    t_p = seq + N_META
    n_p = b_p * t_p

    xp = jnp.concatenate([jnp.broadcast_to(meta_tokens[None], (b_p, N_META, d)), x_prompt], axis=1)
    prefix = state_pool[0]
    xs_ext = jnp.concatenate([jnp.zeros((b_s, 16 - POOL_BUF, d), F32), prefix, x_sample], axis=1)

    h, idx, gates = _pool_layer(xp, xs_ext, pool_w[0], pool_scale[0], ln_g[0, 0], ln_b[0, 0],
                                router_w[0], router_b[0], alpha)
    moe_w = (w_gate, b_gate, w_up, b_up, w_down, b_down)
    h = _moe_layer(h, idx, gates, ln_g[0, 1], ln_b[0, 1], 0, *moe_w, alpha)

    sm = jax.nn.softmax(hg_lb.astype(F32), axis=0)
    lb = (jnp.cumsum(sm, axis=0) - sm[0:1])[1]
    proj = _hgrn_proj(h, hg_w_in[0], lb)
    o, s_p, s_s = _hgrn_recurrence(proj, state_hgrn, b_p, t_p, b_s, t_s)
    h, idx, gates = _hgrn_out(o, proj, hg_norm_g[0], hg_w_out[0], h, ln_g[1, 0], ln_b[1, 0],
                              router_w[1], router_b[1], alpha)
    h = _moe_layer(h, idx, gates, ln_g[1, 1], ln_b[1, 1], 1, *moe_w, alpha)

    y_prompt = h[:n_p].reshape(b_p, t_p, d)[:, N_META:]
    y_sample = h[n_p:].reshape(b_s, t_s, d)
    pool_p = x_prompt[:, seq - POOL_BUF:][None]
    pool_s = jnp.concatenate([prefix, x_sample], axis=1)[:, t_s:][None]
    return (y_prompt, y_sample, pool_p, pool_s, s_p[None], s_s)
```

```python
import functools

import jax
import jax.numpy as jnp
from jax import lax
from jax.experimental import pallas as pl
from jax.experimental.pallas import tpu as pltpu

N_META = 16
POOL_WINDOWS = (2, 4, 8, 16)
POOL_BUF = max(POOL_WINDOWS) - 1
PAST_LEN = 16384
HG_DK = 128
SUB = 16
TOP_K = 4
SWIGLU_LIMIT = 7.0
SWIGLU_ALPHA = 1.702
LN_EPS = 1e-5
RMS_EPS = 1e-6

PAD_TILE = 128
CHUNKS = (512, 256, 128)
SB_ROWS = 1536
FF_TILE = 512
N_FF_TILES = 4
REC_BLOCK = 64
REC_HEADS = 4
REC_SEQS = 8
DISPATCH_TILE = 640
COMBINE_TOKENS = 160
ISSUE_UNROLL = 4
PROJ_SPLIT = 5
OUT_SPLIT = 2
MIB = 1024 * 1024

F32 = jnp.float32
BF16 = jnp.bfloat16


def _cparams(sem, vmem_mib):
    return pltpu.CompilerParams(dimension_semantics=sem, vmem_limit_bytes=vmem_mib * MIB)


def _layer_norm(x, g, b):
    mu = jnp.mean(x, axis=-1, keepdims=True)
    xc = x - mu
    var = jnp.mean(xc * xc, axis=-1, keepdims=True)
    return xc * lax.rsqrt(var + LN_EPS) * g + b


def _dot_3pass(a, b):
    a_hi = a.astype(BF16)
    a_lo = (a - a_hi.astype(F32)).astype(BF16)
    b_hi = b.astype(BF16)
    b_lo = (b - b_hi.astype(F32)).astype(BF16)

    def d(x, y):
        return jnp.dot(x, y, preferred_element_type=F32)
    return d(a_hi, b_hi) + (d(a_hi, b_lo) + d(a_lo, b_hi))


def _route_rows(h, rw, rb):
    logits = _dot_3pass(h, rw) + rb
    n_e = logits.shape[-1]
    lane = lax.broadcasted_iota(jnp.int32, logits.shape, 1)
    vals, ids = [], []
    l = logits
    for _ in range(TOP_K):
        m = jnp.max(l, axis=-1, keepdims=True)
        ix = jnp.min(jnp.where(l == m, lane, n_e), axis=-1, keepdims=True)
        vals.append(m)
        ids.append(ix)
        l = jnp.where(lane == ix, -jnp.inf, l)
    v = jnp.concatenate(vals, axis=1)
    e = jnp.exp(v - vals[0])
    gates = e / jnp.sum(e, axis=-1, keepdims=True)
    return jnp.concatenate(ids, axis=1), gates


def _mix_post(x, y, alpha, g_ref, b_ref, rw_ref, rb_ref, h_ref, idx_ref, gate_ref, rows=slice(None)):
    h = _layer_norm(alpha * x + y, g_ref[...], b_ref[...])
    h_ref[rows, :] = h
    ids, gates = _route_rows(h, rw_ref[...], rb_ref[...])
    idx_ref[rows, :] = ids
    gate_ref[rows, :] = gates


def _window_sum(e, w):
    s = e
    sh = 1
    while sh < w:
        s = s + pltpu.roll(s, sh, axis=0)
        sh *= 2
    return s


def _pool_prompt_kernel(x_ref, halo_ref, pw_ref, ps_ref, g_ref, b_ref, rw_ref, rb_ref,
                        h_ref, idx_ref, gate_ref, *, tt, alpha):
    i = pl.program_id(1)
    cur = x_ref[0]
    halo = jnp.where(i > 0, halo_ref[0], 0.0)
    ext = jnp.concatenate([halo, cur], axis=0)
    grp = cur.shape[1] // len(POOL_WINDOWS)
    pos = i * tt + lax.broadcasted_iota(jnp.int32, (tt, 1), 0)
    ys = []
    for gi, w in enumerate(POOL_WINDOWS):
        s = _window_sum(ext[:, gi * grp:(gi + 1) * grp], w)[16:]
        cnt = jnp.minimum(w, pos + 1).astype(F32)
        d = s / cnt - cur[:, gi * grp:(gi + 1) * grp]
        ys.append(_dot_3pass(d, pw_ref[gi]))
    y = jnp.concatenate(ys, axis=1) * ps_ref[...]
    _mix_post(cur, y, alpha, g_ref, b_ref, rw_ref, rb_ref, h_ref, idx_ref, gate_ref)


def _pool_sample_kernel(x_ref, pw_ref, ps_ref, g_ref, b_ref, rw_ref, rb_ref,
                        h_ref, idx_ref, gate_ref, *, alpha, t_new):
    sb, te, d_model = x_ref.shape
    ext = x_ref[...].reshape(sb * te, d_model)
    grp = d_model // len(POOL_WINDOWS)
    cur = x_ref[:, 16:, :].reshape(sb * t_new, d_model)
    tpos = lax.broadcasted_iota(jnp.int32, (sb, t_new, 1), 1).reshape(sb * t_new, 1)
    ys = []
    for gi, w in enumerate(POOL_WINDOWS):
        s = _window_sum(ext[:, gi * grp:(gi + 1) * grp], w)
        s = s.reshape(sb, te, grp)[:, 16:, :].reshape(sb * t_new, grp)
        cnt = jnp.minimum(w, PAST_LEN + tpos + 1).astype(F32)
        d = s / cnt - cur[:, gi * grp:(gi + 1) * grp]
        ys.append(_dot_3pass(d, pw_ref[gi]))
    y = jnp.concatenate(ys, axis=1) * ps_ref[...]
    _mix_post(cur, y, alpha, g_ref, b_ref, rw_ref, rb_ref, h_ref, idx_ref, gate_ref)


def _const_spec(shape):
    nd = len(shape)
    return pl.BlockSpec(shape, lambda *_: (0,) * nd)


def _pool_layer(xp, xs_ext, pool_w, pool_scale, ln_g, ln_b, router_w, router_b, alpha):
    b_p, t_p, d = xp.shape
    n_e = router_w.shape[1]
    tt = 688
    assert t_p % tt == 0 and tt % 16 == 0
    ps = pool_scale.reshape(1, d)
    g2, b2, rb2 = ln_g.reshape(1, d), ln_b.reshape(1, d), router_b.reshape(1, n_e)
    consts = (pool_w, ps, g2, b2, router_w, rb2)
    const_specs = [_const_spec(c.shape) for c in consts]
    n_p = b_p * t_p
    nt = t_p // tt
    hp, ip, gp = pl.pallas_call(
        functools.partial(_pool_prompt_kernel, tt=tt, alpha=alpha),
        grid=(b_p, nt),
        in_specs=[pl.BlockSpec((1, tt, d), lambda b, i: (b, i, 0)),
                  pl.BlockSpec((1, 16, d), lambda b, i: (b, jnp.maximum(i * (tt // 16) - 1, 0), 0))]
        + const_specs,
        out_specs=[pl.BlockSpec((tt, d), lambda b, i: (b * nt + i, 0)),
                   pl.BlockSpec((tt, TOP_K), lambda b, i: (b * nt + i, 0)),
                   pl.BlockSpec((tt, TOP_K), lambda b, i: (b * nt + i, 0))],
        out_shape=[jax.ShapeDtypeStruct((n_p, d), F32),
                   jax.ShapeDtypeStruct((n_p, TOP_K), jnp.int32),
                   jax.ShapeDtypeStruct((n_p, TOP_K), F32)],
        compiler_params=_cparams(("arbitrary", "arbitrary"), 56),
        name="pool_prompt",
    )(xp, xp, *consts)

    b_s, te, _ = xs_ext.shape
    t_new = te - 16
    sb = 16
    assert b_s % sb == 0
    n_s = b_s * t_new
    hs, is_, gs = pl.pallas_call(
        functools.partial(_pool_sample_kernel, alpha=alpha, t_new=t_new),
        grid=(b_s // sb,),
        in_specs=[pl.BlockSpec((sb, te, d), lambda i: (i, 0, 0))] + const_specs,
        out_specs=[pl.BlockSpec((sb * t_new, d), lambda i: (i, 0)),
                   pl.BlockSpec((sb * t_new, TOP_K), lambda i: (i, 0)),
                   pl.BlockSpec((sb * t_new, TOP_K), lambda i: (i, 0))],
        out_shape=[jax.ShapeDtypeStruct((n_s, d), F32),
                   jax.ShapeDtypeStruct((n_s, TOP_K), jnp.int32),
                   jax.ShapeDtypeStruct((n_s, TOP_K), F32)],
        compiler_params=_cparams(("arbitrary",), 40),
        name="pool_sample",
    )(xs_ext, *consts)
    return (jnp.concatenate([hp, hs], axis=0), jnp.concatenate([ip, is_], axis=0),
            jnp.concatenate([gp, gs], axis=0))


def _routing_tables(idx, n_experts):
    n, k = idx.shape
    a = n * k
    e_flat = idx.reshape(a)
    experts = jnp.arange(n_experts, dtype=jnp.int32)
    onehot = (e_flat[:, None] == experts[None, :]).astype(jnp.int32)
    csum = jnp.cumsum(onehot, axis=0)
    counts = csum[-1]
    pcounts = ((counts + PAD_TILE - 1) // PAD_TILE) * PAD_TILE
    pend = jnp.cumsum(pcounts)
    poff = pend - pcounts
    dest = jnp.sum(onehot * (csum - onehot + poff[None, :]), axis=1)

    m_rows = -(-(a + n_experts * (PAD_TILE - 1)) // PAD_TILE) * PAD_TILE
    n_pad = n_experts * (PAD_TILE - 1)
    n_pad_entries = -(-n_pad // DISPATCH_TILE) * DISPATCH_TILE
    j = jnp.arange(PAD_TILE - 1, dtype=jnp.int32)[None, :]
    trash = m_rows + jnp.arange(n_pad_entries, dtype=jnp.int32)
    pad_dest = jnp.where(j < (pcounts - counts)[:, None], (poff + counts)[:, None] + j,
                         trash[:n_pad].reshape(n_experts, PAD_TILE - 1)).reshape(n_pad)
    dest_all = jnp.concatenate([dest, pad_dest, trash[n_pad:]]).astype(jnp.int32)
    m_total = m_rows + n_pad_entries

    sb_rows = SB_ROWS
    s_max = n_experts + m_rows // sb_rows
    nsb = (pcounts + sb_rows - 1) // sb_rows
    sb_end = jnp.cumsum(nsb)
    n_used = sb_end[-1]
    s = jnp.arange(s_max, dtype=jnp.int32)
    s_eff = jnp.minimum(s, n_used - 1)
    e_of_s = jnp.sum((s_eff[:, None] >= sb_end[None, :]).astype(jnp.int32), axis=1)
    e_of_s = jnp.minimum(e_of_s, n_experts - 1)
    oh_s = (e_of_s[:, None] == experts[None, :]).astype(jnp.int32)
    local = s_eff - jnp.sum(oh_s * (sb_end - nsb)[None, :], axis=1)
    row0 = jnp.sum(oh_s * poff[None, :], axis=1) + local * sb_rows
    rows = jnp.clip(jnp.sum(oh_s * pcounts[None, :], axis=1) - local * sb_rows, 0, sb_rows)
    rows = jnp.where(s < n_used, rows, 0)
    return dict(dest=dest.astype(jnp.int32), dest_all=dest_all, m_total=m_total,
                sb_e=e_of_s.astype(jnp.int32), sb_row0=row0.astype(jnp.int32),
                sb_rows=rows.astype(jnp.int32), n_used=n_used.reshape(1).astype(jnp.int32),
                s_max=s_max, n_main=a)


def _dispatch_kernel(dest_ref, h_ref, xs_ref, z_scr, sem, *, n_main_steps):
    i = pl.program_id(0)
    base = i * DISPATCH_TILE

    def row_copy(src_ref, src_row, a):
        return pltpu.make_async_copy(src_ref.at[pl.ds(src_row, 1)],
                                     xs_ref.at[pl.ds(dest_ref[base + a], 1)], sem)

    @pl.when(i == 0)
    def _():
        z_scr[...] = jnp.zeros_like(z_scr)

    @pl.when(i < n_main_steps)
    def _():
        def issue(t, c):
            for k in range(TOP_K):
                row_copy(h_ref, t, t * TOP_K + k).start()
            return c
        lax.fori_loop(0, DISPATCH_TILE // TOP_K, issue, 0, unroll=ISSUE_UNROLL)

    @pl.when(i >= n_main_steps)
    def _():
        def issue(a, c):
            row_copy(z_scr, 0, a).start()
            return c
        lax.fori_loop(0, DISPATCH_TILE, issue, 0, unroll=4 * ISSUE_UNROLL)

    all_rows = xs_ref.at[pl.ds(0, DISPATCH_TILE)]
    pltpu.make_async_copy(all_rows, all_rows, sem).wait()


def _dispatch(h, tabs):
    n, d = h.shape
    n_main = tabs["n_main"]
    assert n_main % DISPATCH_TILE == 0
    n_steps = tabs["dest_all"].shape[0] // DISPATCH_TILE
    n_main_steps = n_main // DISPATCH_TILE
    tt = DISPATCH_TILE // TOP_K
    return pl.pallas_call(
        functools.partial(_dispatch_kernel, n_main_steps=n_main_steps),
        grid_spec=pltpu.PrefetchScalarGridSpec(
            num_scalar_prefetch=1, grid=(n_steps,),
            in_specs=[pl.BlockSpec((tt, d), lambda i, dr: (jnp.minimum(i, n_main_steps - 1), 0))],
            out_specs=pl.BlockSpec(memory_space=pl.ANY),
            scratch_shapes=[pltpu.VMEM((8, d), F32), pltpu.SemaphoreType.DMA(())]),
        out_shape=jax.ShapeDtypeStruct((tabs["m_total"], d), F32),
        compiler_params=_cparams(("arbitrary",), 16),
        name="moe_dispatch",
    )(tabs["dest_all"], h)


def _expert_kernel(sbe_ref, row0_ref, rows_ref, nused_ref, xs_ref, wg_ref, wu_ref, wd_ref,
                   bg_ref, bu_ref, bd_ref, ys_ref, x_scr, h_scr, y_scr, x_sem, y_sem):
    del sbe_ref, nused_ref
    s = pl.program_id(0)
    j = pl.program_id(1)
    rows = rows_ref[s]
    row0 = pl.multiple_of(row0_ref[s], PAD_TILE)
    big = CHUNKS[0]
    n_big = rows // big

    def for_chunks(fn):
        def body(c, carry):
            fn(pl.multiple_of(c * big, big), big)
            return carry
        lax.fori_loop(0, n_big, body, 0)
        off = n_big * big
        rem = rows - off
        for tm in CHUNKS[1:]:
            has = (rem & tm) != 0

            @pl.when(has)
            def _(off=off, tm=tm):
                fn(pl.multiple_of(off, tm), tm)
            off = off + jnp.where(has, tm, 0)

    def x_piece(p, base=row0):
        r = pl.multiple_of(p * PAD_TILE, PAD_TILE)
        return pltpu.make_async_copy(xs_ref.at[pl.ds(pl.multiple_of(base + r, PAD_TILE), PAD_TILE)],
                                     x_scr.at[pl.ds(r, PAD_TILE)], x_sem.at[p])

    def start_x(base, n_rows):
        def start(p, carry):
            x_piece(p, base).start()
            return carry
        lax.fori_loop(0, n_rows // PAD_TILE, start, 0)

    def y_copy(n, slot, r0, tm):
        return pltpu.make_async_copy(
            y_scr.at[slot, pl.ds(r0, tm)],
            ys_ref.at[pl.ds(pl.multiple_of(row0 + r0, PAD_TILE), tm), pl.ds(n * FF_TILE, FF_TILE)],
            y_sem.at[slot])

    @pl.when(rows > 0)
    def _():
        @pl.when((j == 0) & (s == 0))
        def _():
            start_x(row0, rows)

        @pl.when(j == N_FF_TILES)
        def _():
            n_sb = pl.num_programs(0)
            s_next = jnp.minimum(s + 1, n_sb - 1)
            rows_next = jnp.where(s + 1 < n_sb, rows_ref[s_next], 0)
            start_x(pl.multiple_of(row0_ref[s_next], PAD_TILE), rows_next)

        @pl.when(j < N_FF_TILES)
        def _():
            def gate_up(r0, tm):
                @pl.when(j == 0)
                def _():
                    for q in range(tm // PAD_TILE):
                        x_piece(r0 // PAD_TILE + q).wait()
                x = x_scr[pl.ds(r0, tm), :]
                hg = jnp.dot(x, wg_ref[0], preferred_element_type=F32) + bg_ref[0]
                hu = jnp.dot(x, wu_ref[0], preferred_element_type=F32) + bu_ref[0]
                hg = jnp.minimum(hg, SWIGLU_LIMIT)
                hu = jnp.clip(hu, -SWIGLU_LIMIT, SWIGLU_LIMIT)
                act = (hu + 1.0) * hg * jax.nn.sigmoid(SWIGLU_ALPHA * hg)
                h_scr[j, pl.ds(r0, tm), :] = act.astype(BF16)
            for_chunks(gate_up)

        @pl.when(j >= N_FF_TILES)
        def _():
            slot = (j - N_FF_TILES) % 2

            def down(r0, tm):
                acc = bd_ref[0] + jnp.zeros((tm, FF_TILE), F32)
                for f in range(N_FF_TILES):
                    w = wd_ref[0, f * FF_TILE:(f + 1) * FF_TILE, :].astype(BF16)
                    acc = acc + jnp.dot(h_scr[f, pl.ds(r0, tm), :], w, preferred_element_type=F32)
                y_scr[slot, pl.ds(r0, tm), :] = acc
                for n in range(N_FF_TILES):
                    @pl.when(j == N_FF_TILES + n)
                    def _(n=n):
                        y_copy(n, n % 2, r0, tm).start()
            for_chunks(down)

            @pl.when(j > N_FF_TILES)
            def _():
                for_chunks(lambda r0, tm: y_copy(0, 1 - slot, r0, tm).wait())

            @pl.when(j == 2 * N_FF_TILES - 1)
            def _():
                for_chunks(lambda r0, tm: y_copy(0, slot, r0, tm).wait())


def _experts(xs, tabs, layer, w_gate, b_gate, w_up, b_up, w_down, b_down):
    n_l, n_e, d, f = w_gate.shape
    assert f == N_FF_TILES * FF_TILE and d == N_FF_TILES * FF_TILE
    w_gate, w_up, w_down = (w.reshape(n_l * n_e, *w.shape[2:]) for w in (w_gate, w_up, w_down))
    b_gate, b_up, b_down = (b.reshape(n_l * n_e, 1, b.shape[2]) for b in (b_gate, b_up, b_down))
    m_total = xs.shape[0]
    sb_rows = SB_ROWS
    last = N_FF_TILES - 1

    def up_map(s, j, sbe, r0, nrows, nu):
        return (sbe[s], 0, jnp.where(s < nu[0], jnp.minimum(j, last), last))

    def down_map(s, j, sbe, r0, nrows, nu):
        return (sbe[s], 0, jnp.where(s < nu[0], jnp.maximum(j - N_FF_TILES, 0), last))

    return pl.pallas_call(
        _expert_kernel,
        grid_spec=pltpu.PrefetchScalarGridSpec(
            num_scalar_prefetch=4, grid=(tabs["s_max"], 2 * N_FF_TILES),
            in_specs=[pl.BlockSpec(memory_space=pl.ANY),
                      pl.BlockSpec((1, d, FF_TILE), up_map),
                      pl.BlockSpec((1, d, FF_TILE), up_map),
                      pl.BlockSpec((1, f, FF_TILE), down_map),
                      pl.BlockSpec((1, 1, FF_TILE), up_map),
                      pl.BlockSpec((1, 1, FF_TILE), up_map),
                      pl.BlockSpec((1, 1, FF_TILE), down_map)],
            out_specs=pl.BlockSpec(memory_space=pl.ANY),
            scratch_shapes=[pltpu.VMEM((sb_rows, d), F32),
                            pltpu.VMEM((N_FF_TILES, sb_rows, FF_TILE), BF16),
                            pltpu.VMEM((2, sb_rows, FF_TILE), F32),
                            pltpu.SemaphoreType.DMA((SB_ROWS // PAD_TILE,)),
                            pltpu.SemaphoreType.DMA((2,))]),
        out_shape=jax.ShapeDtypeStruct((m_total, d), F32),
        compiler_params=_cparams(("arbitrary", "arbitrary"), 56),
        name="moe_experts",
    )(tabs["sb_e"] + layer * n_e, tabs["sb_row0"], tabs["sb_rows"], tabs["n_used"], xs,
      w_gate, w_up, w_down, b_gate, b_up, b_down)


def _combine_kernel(dest_ref, ys_ref, h_ref, gate_ref, g_ref, b_ref, o_ref, buf, sem, *, alpha):
    i = pl.program_id(0)
    n_steps = pl.num_programs(0)
    tt = COMBINE_TOKENS

    def row_copy(step, slot, t, k):
        return pltpu.make_async_copy(
            ys_ref.at[pl.ds(dest_ref[(step * tt + t) * TOP_K + k], 1)],
            buf.at[slot, k, pl.ds(t, 1)], sem.at[slot])

    def issue(step, slot):
        def body(t, c):
            for k in range(TOP_K):
                row_copy(step, slot, t, k).start()
            return c
        lax.fori_loop(0, tt, body, 0, unroll=ISSUE_UNROLL)

    @pl.when(i == 0)
    def _():
        issue(0, 0)

    @pl.when(i + 1 < n_steps)
    def _():
        issue(i + 1, (i + 1) % 2)

    slot = i % 2

    pltpu.make_async_copy(buf.at[slot], buf.at[slot], sem.at[slot]).wait()

    gates = gate_ref[...]
    y = gates[:, 0:1] * buf[slot, 0]
    for k in range(1, TOP_K):
        y = y + gates[:, k:k + 1] * buf[slot, k]
    o_ref[...] = _layer_norm(alpha * h_ref[...] + y, g_ref[...], b_ref[...])


def _combine(ys, dest, h, gates, ln_g, ln_b, alpha):
    n, d = h.shape
    tt = COMBINE_TOKENS
    assert n % tt == 0
    return pl.pallas_call(
        functools.partial(_combine_kernel, alpha=alpha),
        grid_spec=pltpu.PrefetchScalarGridSpec(
            num_scalar_prefetch=1, grid=(n // tt,),
            in_specs=[pl.BlockSpec(memory_space=pl.ANY),
                      pl.BlockSpec((tt, d), lambda i, dr: (i, 0)),
                      pl.BlockSpec((tt, TOP_K), lambda i, dr: (i, 0)),
                      pl.BlockSpec((1, d), lambda i, dr: (0, 0)),
                      pl.BlockSpec((1, d), lambda i, dr: (0, 0))],
            out_specs=pl.BlockSpec((tt, d), lambda i, dr: (i, 0)),
            scratch_shapes=[pltpu.VMEM((2, TOP_K, tt, d), F32),
                            pltpu.SemaphoreType.DMA((2,))]),
        out_shape=jax.ShapeDtypeStruct((n, d), F32),
        compiler_params=_cparams(("arbitrary",), 40),
        name="moe_combine",
    )(dest, ys, h, gates, ln_g.reshape(1, d), ln_b.reshape(1, d))


def _moe_layer(h, idx, gates, ln_g, ln_b, layer, w_gate, b_gate, w_up, b_up, w_down, b_down, alpha):
    tabs = _routing_tables(idx, w_gate.shape[1])
    xs = _dispatch(h, tabs)
    ys = _experts(xs, tabs, layer, w_gate, b_gate, w_up, b_up, w_down, b_down)
    return _combine(ys, tabs["dest"], h, gates, ln_g, ln_b, alpha)


def _proj_kernel(x_ref, w_ref, lb_ref, o_ref, *, tiles_per_section):
    sec = pl.program_id(1) // tiles_per_section
    lb = lb_ref[...]
    sub = x_ref.shape[0] // PROJ_SPLIT
    for c in range(PROJ_SPLIT):
        rows = slice(c * sub, (c + 1) * sub)
        p = jnp.dot(x_ref[rows, :], w_ref[...], preferred_element_type=F32)
        sig = jax.nn.sigmoid(p)
        forget = lb + (1.0 - lb) * sig
        o_ref[rows, :] = jnp.where(sec == 1, forget, jnp.where(sec == 2, p, p * sig))


def _hgrn_proj(h, w_in, lb):
    n, d = h.shape
    tn = 512
    tm = 1160
    assert n % tm == 0 and d % tn == 0 and w_in.shape[1] == 4 * d and tm % (8 * PROJ_SPLIT) == 0
    tps = d // tn
    return pl.pallas_call(
        functools.partial(_proj_kernel, tiles_per_section=tps),
        grid=(n // tm, 4 * tps),
        in_specs=[pl.BlockSpec((tm, d), lambda i, j: (i, 0)),
                  pl.BlockSpec((d, tn), lambda i, j: (0, j)),
                  pl.BlockSpec((1, tn), lambda i, j: (0, j % tps))],
        out_specs=pl.BlockSpec((tm, tn), lambda i, j: (i, j)),
        out_shape=jax.ShapeDtypeStruct((n, 4 * d), F32),
        compiler_params=_cparams(("arbitrary", "arbitrary"), 48),
        name="hgrn_proj",
    )(h, w_in, lb.reshape(1, d))


_NT = (((1,), (1,)), ((), ()))
_TN = (((0,), (0,)), ((), ()))


def _hgrn_block(q, f, v, st):
    c = q.shape[0]
    n_sub = c // SUB
    lf = jnp.log(f)
    k = 1.0 - f
    row = lax.broadcasted_iota(jnp.int32, (c, 1), 0)
    loc = row % SUB
    bl = lf
    for sh in (1, 2, 4, 8):
        bl = bl + jnp.where(loc >= sh, pltpu.roll(bl, sh, axis=0), 0.0)
    tot = [bl[SUB * i + SUB - 1:SUB * i + SUB, :] for i in range(n_sub)]
    off = [jnp.zeros_like(tot[0])]
    for i in range(1, n_sub):
        off.append(off[-1] + tot[i - 1])
    b_last = off[-1] + tot[-1]
    if n_sub > 1:
        blk = row // SUB
        offs = off[0]
        tots = tot[0]
        for i in range(1, n_sub):
            offs = jnp.where(blk == i, off[i], offs)
            tots = jnp.where(blk == i, tot[i], tots)
        b = bl + offs
    else:
        b = bl
        tots = tot[0]
    r2 = lax.broadcasted_iota(jnp.int32, (c, c), 0)
    c2 = lax.broadcasted_iota(jnp.int32, (c, c), 1)
    a_loc = lax.dot_general(q * jnp.exp(bl), k * jnp.exp(-bl), _NT, preferred_element_type=F32)
    a = jnp.where((r2 // SUB == c2 // SUB) & (c2 <= r2), a_loc, 0.0)
    if n_sub > 1:
        k_end = k * jnp.exp(tots - bl)
        for jb in range(n_sub - 1):
            gamma = off[jb] + tot[jb]
            q_rel = q * jnp.exp(jnp.minimum(b - gamma, 0.0))
            a_j = lax.dot_general(q_rel, k_end, _NT, preferred_element_type=F32)
            a = jnp.where((c2 // SUB == jb) & (r2 // SUB > jb), a_j, a)
    o = jnp.dot(a, v, preferred_element_type=F32)
    o = o + lax.dot_general(q * jnp.exp(b), st, _NT, preferred_element_type=F32)
    kd = k * jnp.exp(b_last - b)
    st_new = st * jnp.exp(b_last) + lax.dot_general(v, kd, _TN, preferred_element_type=F32)
    o = o * lax.rsqrt(jnp.mean(o * o, axis=-1, keepdims=True) + RMS_EPS)
    return o, st_new


def _rec_prompt_kernel(q_ref, f_ref, v_ref, o_ref, s_ref, st_scr, *, n_blocks):
    st_scr[...] = jnp.zeros_like(st_scr)

    def run(r0, c):
        rows = pl.ds(r0, c)
        for hd in range(REC_HEADS):
            cols = slice(hd * HG_DK, (hd + 1) * HG_DK)
            o, st = _hgrn_block(q_ref[rows, cols], f_ref[rows, cols], v_ref[rows, cols], st_scr[hd])
            o_ref[rows, cols] = o
            st_scr[hd] = st

    run(0, SUB)

    def body(i, carry):
        run(pl.multiple_of(SUB + i * REC_BLOCK, SUB), REC_BLOCK)
        return carry
    lax.fori_loop(0, n_blocks, body, 0, unroll=2)
    for hd in range(REC_HEADS):
        s_ref[0, hd] = st_scr[hd].T


def _rec_sample_kernel(q_ref, f_ref, v_ref, s0_ref, o_ref, s_ref, *, t_new):
    sb = s0_ref.shape[1]

    def body(i, carry):
        for u in range(REC_SEQS):
            sq = i * REC_SEQS + u
            rows = pl.ds(pl.multiple_of(sq * t_new, t_new), t_new)
            o, st = _hgrn_block_short(q_ref[rows, :], f_ref[rows, :], v_ref[rows, :],
                                      s0_ref[0, sq, 0].T)
            o_ref[rows, :] = o
            s_ref[0, sq, 0] = st.T
        return carry
    lax.fori_loop(0, sb // REC_SEQS, body, 0)


def _hgrn_block_short(q, f, v, st):
    c = q.shape[0]
    lf = jnp.log(f)
    k = 1.0 - f
    row = lax.broadcasted_iota(jnp.int32, (c, 1), 0)
    bl = lf
    sh = 1
    while sh < c:
        bl = bl + jnp.where(row >= sh, pltpu.roll(bl, sh, axis=0), 0.0)
        sh *= 2
    b_last = bl[c - 1:c, :]
    r2 = lax.broadcasted_iota(jnp.int32, (c, c), 0)
    c2 = lax.broadcasted_iota(jnp.int32, (c, c), 1)
    qe = q * jnp.exp(bl)
    a = lax.dot_general(qe, k * jnp.exp(-bl), _NT, preferred_element_type=F32)
    a = jnp.where(c2 <= r2, a, 0.0)
    o = jnp.dot(a, v, preferred_element_type=F32)
    o = o + lax.dot_general(qe, st, _NT, preferred_element_type=F32)
    kd = k * jnp.exp(b_last - bl)
    st_new = st * jnp.exp(b_last) + lax.dot_general(v, kd, _TN, preferred_element_type=F32)
    o = o * lax.rsqrt(jnp.mean(o * o, axis=-1, keepdims=True) + RMS_EPS)
    return o, st_new


def _hgrn_recurrence(proj, state_s, b_p, t_p, b_s, t_s):
    n, d4 = proj.shape
    d = d4 // 4
    heads = d // HG_DK
    n_p = b_p * t_p
    assert (t_p - SUB) % REC_BLOCK == 0 and t_s < SUB and t_s % 8 == 0
    assert heads % REC_HEADS == 0
    hgroups = heads // REC_HEADS
    gw = REC_HEADS * HG_DK
    o_p, s_p = pl.pallas_call(
        functools.partial(_rec_prompt_kernel, n_blocks=(t_p - SUB) // REC_BLOCK),
        grid=(b_p, hgroups),
        in_specs=[pl.BlockSpec((t_p, gw), lambda b, h: (b, h)),
                  pl.BlockSpec((t_p, gw), lambda b, h: (b, hgroups + h)),
                  pl.BlockSpec((t_p, gw), lambda b, h: (b, 2 * hgroups + h))],
        out_specs=[pl.BlockSpec((t_p, gw), lambda b, h: (b, h)),
                   pl.BlockSpec((1, REC_HEADS, HG_DK, HG_DK), lambda b, h: (b, h, 0, 0))],
        out_shape=[jax.ShapeDtypeStruct((n_p, d), F32),
                   jax.ShapeDtypeStruct((b_p, heads, HG_DK, HG_DK), F32)],
        scratch_shapes=[pltpu.VMEM((REC_HEADS, HG_DK, HG_DK), F32)],
        compiler_params=_cparams(("arbitrary", "arbitrary"), 48),
        name="hgrn_rec_prompt",
    )(proj, proj, proj)

    proj_s = proj[n_p:]
    sb = 32
    assert b_s % sb == 0
    o_s, s_s = pl.pallas_call(
        functools.partial(_rec_sample_kernel, t_new=t_s),
        grid=(heads, b_s // sb),
        in_specs=[pl.BlockSpec((sb * t_s, HG_DK), lambda h, i: (i, h)),
                  pl.BlockSpec((sb * t_s, HG_DK), lambda h, i: (i, heads + h)),
                  pl.BlockSpec((sb * t_s, HG_DK), lambda h, i: (i, 2 * heads + h)),
                  pl.BlockSpec((1, sb, 1, HG_DK, HG_DK), lambda h, i: (0, i, h, 0, 0))],
        out_specs=[pl.BlockSpec((sb * t_s, HG_DK), lambda h, i: (i, h)),
                   pl.BlockSpec((1, sb, 1, HG_DK, HG_DK), lambda h, i: (0, i, h, 0, 0))],
        out_shape=[jax.ShapeDtypeStruct((b_s * t_s, d), F32),
                   jax.ShapeDtypeStruct((1, b_s, heads, HG_DK, HG_DK), F32)],
        compiler_params=_cparams(("arbitrary", "arbitrary"), 32),
        name="hgrn_rec_sample",
    )(proj_s, proj_s, proj_s, state_s)
    return jnp.concatenate([o_p, o_s], axis=0), s_p, s_s


def _hgrn_out_kernel(o_ref, sg_ref, ng_ref, w_ref, x_ref, g_ref, b_ref, rw_ref, rb_ref,
                     h_ref, idx_ref, gate_ref, *, alpha):
    sub = o_ref.shape[0] // OUT_SPLIT
    for c in range(OUT_SPLIT):
        rows = slice(c * sub, (c + 1) * sub)
        z = (o_ref[rows, :] * ng_ref[...] * sg_ref[rows, :]).astype(BF16)
        y = jnp.dot(z, w_ref[...], preferred_element_type=F32)
        _mix_post(x_ref[rows, :], y, alpha, g_ref, b_ref, rw_ref, rb_ref, h_ref, idx_ref, gate_ref, rows)


def _hgrn_out(o, proj, norm_g, w_out, x, ln_g, ln_b, router_w, router_b, alpha):
    n, d = x.shape
    n_e = router_w.shape[1]
    tm = 464
    assert n % tm == 0 and tm % (8 * OUT_SPLIT) == 0
    consts = (ln_g.reshape(1, d), ln_b.reshape(1, d), router_w, router_b.reshape(1, n_e))
    row = lambda i: (i, 0)
    return pl.pallas_call(
        functools.partial(_hgrn_out_kernel, alpha=alpha),
        grid=(n // tm,),
        in_specs=[pl.BlockSpec((tm, d), row),
                  pl.BlockSpec((tm, d), lambda i: (i, 3)),
                  _const_spec((1, d)),
                  _const_spec((d, d)),
                  pl.BlockSpec((tm, d), row)] + [_const_spec(c.shape) for c in consts],
        out_specs=[pl.BlockSpec((tm, d), row), pl.BlockSpec((tm, TOP_K), row),
                   pl.BlockSpec((tm, TOP_K), row)],
        out_shape=[jax.ShapeDtypeStruct((n, d), F32),
                   jax.ShapeDtypeStruct((n, TOP_K), jnp.int32),
                   jax.ShapeDtypeStruct((n, TOP_K), F32)],
        compiler_params=_cparams(("arbitrary",), 56),
        name="hgrn_out",
    )(o, proj, norm_g.reshape(1, d), w_out.astype(BF16), x, *consts)


def kernel(x_prompt, x_sample, state_pool, state_hgrn, meta_tokens, pool_w, pool_scale, hg_w_in, hg_lb, hg_norm_g, hg_w_out, ln_g, ln_b, router_w, router_b, w_gate, b_gate, w_up, b_up, w_down, b_down):
    depth = ln_g.shape[0]
    assert depth == 2, "layer 0 = pooling mixer, layer 1 = HGRN2 mixer"
    alpha = float((2 * depth) ** 0.25)
    b_p, seq, d = x_prompt.shape
    b_s, t_s, _ = x_sample.shape
    t_p = seq + N_META
    n_p = b_p * t_p

    xp = jnp.concatenate([jnp.broadcast_to(meta_tokens[None], (b_p, N_META, d)), x_prompt], axis=1)
    prefix = state_pool[0]
    xs_ext = jnp.concatenate([jnp.zeros((b_s, 16 - POOL_BUF, d), F32), prefix, x_sample], axis=1)

    h, idx, gates = _pool_layer(xp, xs_ext, pool_w[0], pool_scale[0], ln_g[0, 0], ln_b[0, 0],
                                router_w[0], router_b[0], alpha)
    moe_w = (w_gate, b_gate, w_up, b_up, w_down, b_down)
    h = _moe_layer(h, idx, gates, ln_g[0, 1], ln_b[0, 1], 0, *moe_w, alpha)

    sm = jax.nn.softmax(hg_lb.astype(F32), axis=0)
    lb = (jnp.cumsum(sm, axis=0) - sm[0:1])[1]
    proj = _hgrn_proj(h, hg_w_in[0], lb)
    o, s_p, s_s = _hgrn_recurrence(proj, state_hgrn, b_p, t_p, b_s, t_s)
    h, idx, gates = _hgrn_out(o, proj, hg_norm_g[0], hg_w_out[0], h, ln_g[1, 0], ln_b[1, 0],
                              router_w[1], router_b[1], alpha)
    h = _moe_layer(h, idx, gates, ln_g[1, 1], ln_b[1, 1], 1, *moe_w, alpha)

    y_prompt = h[:n_p].reshape(b_p, t_p, d)[:, N_META:]
    y_sample = h[n_p:].reshape(b_s, t_s, d)
    pool_p = x_prompt[:, seq - POOL_BUF:][None]
    pool_s = jnp.concatenate([prefix, x_sample], axis=1)[:, t_s:][None]
    return (y_prompt, y_sample, pool_p, pool_s, s_p[None], s_s)
```

```python
import functools

import jax
import jax.numpy as jnp
from jax import lax
from jax.experimental import pallas as pl
from jax.experimental.pallas import tpu as pltpu

N_META = 16
POOL_WINDOWS = (2, 4, 8, 16)
POOL_BUF = max(POOL_WINDOWS) - 1
PAST_LEN = 16384
HG_DK = 128
SUB = 16
TOP_K = 4
SWIGLU_LIMIT = 7.0
SWIGLU_ALPHA = 1.702
LN_EPS = 1e-5
RMS_EPS = 1e-6

PAD_TILE = 128
CHUNKS = (512, 256, 128)
SB_ROWS = 1536
FF_TILE = 512
N_FF_TILES = 4
REC_BLOCK = 64
REC_HEADS = 4
REC_SEQS = 8
DISPATCH_TILE = 640
COMBINE_TOKENS = 160
ISSUE_UNROLL = 4
PROJ_SPLIT = 2
OUT_SPLIT = 2
MIB = 1024 * 1024

F32 = jnp.float32
BF16 = jnp.bfloat16


def _cparams(sem, vmem_mib):
    return pltpu.CompilerParams(dimension_semantics=sem, vmem_limit_bytes=vmem_mib * MIB)


def _layer_norm(x, g, b):
    mu = jnp.mean(x, axis=-1, keepdims=True)
    xc = x - mu
    var = jnp.mean(xc * xc, axis=-1, keepdims=True)
    return xc * lax.rsqrt(var + LN_EPS) * g + b


def _dot_3pass(a, b):
    a_hi = a.astype(BF16)
    a_lo = (a - a_hi.astype(F32)).astype(BF16)
    b_hi = b.astype(BF16)
    b_lo = (b - b_hi.astype(F32)).astype(BF16)

    def d(x, y):
        return jnp.dot(x, y, preferred_element_type=F32)
    return d(a_hi, b_hi) + (d(a_hi, b_lo) + d(a_lo, b_hi))


def _route_rows(h, rw, rb):
    logits = _dot_3pass(h, rw) + rb
    n_e = logits.shape[-1]
    lane = lax.broadcasted_iota(jnp.int32, logits.shape, 1)
    vals, ids = [], []
    l = logits
    for _ in range(TOP_K):
        m = jnp.max(l, axis=-1, keepdims=True)
        ix = jnp.min(jnp.where(l == m, lane, n_e), axis=-1, keepdims=True)
        vals.append(m)
        ids.append(ix)
        l = jnp.where(lane == ix, -jnp.inf, l)
    v = jnp.concatenate(vals, axis=1)
    e = jnp.exp(v - vals[0])
    gates = e / jnp.sum(e, axis=-1, keepdims=True)
    return jnp.concatenate(ids, axis=1), gates


def _mix_post(x, y, alpha, g_ref, b_ref, rw_ref, rb_ref, h_ref, idx_ref, gate_ref, rows=slice(None)):
    h = _layer_norm(alpha * x + y, g_ref[...], b_ref[...])
    h_ref[rows, :] = h
    ids, gates = _route_rows(h, rw_ref[...], rb_ref[...])
    idx_ref[rows, :] = ids
    gate_ref[rows, :] = gates


def _window_sum(e, w):
    s = e
    sh = 1
    while sh < w:
        s = s + pltpu.roll(s, sh, axis=0)
        sh *= 2
    return s


def _pool_prompt_kernel(x_ref, halo_ref, pw_ref, ps_ref, g_ref, b_ref, rw_ref, rb_ref,
                        h_ref, idx_ref, gate_ref, *, tt, alpha):
    i = pl.program_id(1)
    cur = x_ref[0]
    halo = jnp.where(i > 0, halo_ref[0], 0.0)
    ext = jnp.concatenate([halo, cur], axis=0)
    grp = cur.shape[1] // len(POOL_WINDOWS)
    pos = i * tt + lax.broadcasted_iota(jnp.int32, (tt, 1), 0)
    ys = []
    for gi, w in enumerate(POOL_WINDOWS):
        s = _window_sum(ext[:, gi * grp:(gi + 1) * grp], w)[16:]
        cnt = jnp.minimum(w, pos + 1).astype(F32)
        d = s / cnt - cur[:, gi * grp:(gi + 1) * grp]
        ys.append(_dot_3pass(d, pw_ref[gi]))
    y = jnp.concatenate(ys, axis=1) * ps_ref[...]
    _mix_post(cur, y, alpha, g_ref, b_ref, rw_ref, rb_ref, h_ref, idx_ref, gate_ref)


def _pool_sample_kernel(x_ref, pw_ref, ps_ref, g_ref, b_ref, rw_ref, rb_ref,
                        h_ref, idx_ref, gate_ref, *, alpha, t_new):
    sb, te, d_model = x_ref.shape
    ext = x_ref[...].reshape(sb * te, d_model)
    grp = d_model // len(POOL_WINDOWS)
    cur = x_ref[:, 16:, :].reshape(sb * t_new, d_model)
    tpos = lax.broadcasted_iota(jnp.int32, (sb, t_new, 1), 1).reshape(sb * t_new, 1)
    ys = []
    for gi, w in enumerate(POOL_WINDOWS):
        s = _window_sum(ext[:, gi * grp:(gi + 1) * grp], w)
        s = s.reshape(sb, te, grp)[:, 16:, :].reshape(sb * t_new, grp)
        cnt = jnp.minimum(w, PAST_LEN + tpos + 1).astype(F32)
        d = s / cnt - cur[:, gi * grp:(gi + 1) * grp]
        ys.append(_dot_3pass(d, pw_ref[gi]))
    y = jnp.concatenate(ys, axis=1) * ps_ref[...]
    _mix_post(cur, y, alpha, g_ref, b_ref, rw_ref, rb_ref, h_ref, idx_ref, gate_ref)


def _const_spec(shape):
    nd = len(shape)
    return pl.BlockSpec(shape, lambda *_: (0,) * nd)


def _pool_layer(xp, xs_ext, pool_w, pool_scale, ln_g, ln_b, router_w, router_b, alpha):
    b_p, t_p, d = xp.shape
    n_e = router_w.shape[1]
    tt = 688
    assert t_p % tt == 0 and tt % 16 == 0
    ps = pool_scale.reshape(1, d)
    g2, b2, rb2 = ln_g.reshape(1, d), ln_b.reshape(1, d), router_b.reshape(1, n_e)
    consts = (pool_w, ps, g2, b2, router_w, rb2)
    const_specs = [_const_spec(c.shape) for c in consts]
    n_p = b_p * t_p
    nt = t_p // tt
    hp, ip, gp = pl.pallas_call(
        functools.partial(_pool_prompt_kernel, tt=tt, alpha=alpha),
        grid=(b_p, nt),
        in_specs=[pl.BlockSpec((1, tt, d), lambda b, i: (b, i, 0)),
                  pl.BlockSpec((1, 16, d), lambda b, i: (b, jnp.maximum(i * (tt // 16) - 1, 0), 0))]
        + const_specs,
        out_specs=[pl.BlockSpec((tt, d), lambda b, i: (b * nt + i, 0)),
                   pl.BlockSpec((tt, TOP_K), lambda b, i: (b * nt + i, 0)),
                   pl.BlockSpec((tt, TOP_K), lambda b, i: (b * nt + i, 0))],
        out_shape=[jax.ShapeDtypeStruct((n_p, d), F32),
                   jax.ShapeDtypeStruct((n_p, TOP_K), jnp.int32),
                   jax.ShapeDtypeStruct((n_p, TOP_K), F32)],
        compiler_params=_cparams(("arbitrary", "arbitrary"), 56),
        name="pool_prompt",
    )(xp, xp, *consts)

    b_s, te, _ = xs_ext.shape
    t_new = te - 16
    sb = 16
    assert b_s % sb == 0
    n_s = b_s * t_new
    hs, is_, gs = pl.pallas_call(
        functools.partial(_pool_sample_kernel, alpha=alpha, t_new=t_new),
        grid=(b_s // sb,),
        in_specs=[pl.BlockSpec((sb, te, d), lambda i: (i, 0, 0))] + const_specs,
        out_specs=[pl.BlockSpec((sb * t_new, d), lambda i: (i, 0)),
                   pl.BlockSpec((sb * t_new, TOP_K), lambda i: (i, 0)),
                   pl.BlockSpec((sb * t_new, TOP_K), lambda i: (i, 0))],
        out_shape=[jax.ShapeDtypeStruct((n_s, d), F32),
                   jax.ShapeDtypeStruct((n_s, TOP_K), jnp.int32),
                   jax.ShapeDtypeStruct((n_s, TOP_K), F32)],
        compiler_params=_cparams(("arbitrary",), 40),
        name="pool_sample",
    )(xs_ext, *consts)
    return (jnp.concatenate([hp, hs], axis=0), jnp.concatenate([ip, is_], axis=0),
            jnp.concatenate([gp, gs], axis=0))


def _routing_tables(idx, n_experts):
    n, k = idx.shape
    a = n * k
    e_flat = idx.reshape(a)
    experts = jnp.arange(n_experts, dtype=jnp.int32)
    onehot = (e_flat[:, None] == experts[None, :]).astype(jnp.int32)
    csum = jnp.cumsum(onehot, axis=0)
    counts = csum[-1]
    pcounts = ((counts + PAD_TILE - 1) // PAD_TILE) * PAD_TILE
    pend = jnp.cumsum(pcounts)
    poff = pend - pcounts
    dest = jnp.sum(onehot * (csum - onehot + poff[None, :]), axis=1)

    m_rows = -(-(a + n_experts * (PAD_TILE - 1)) // PAD_TILE) * PAD_TILE
    n_pad = n_experts * (PAD_TILE - 1)
    n_pad_entries = -(-n_pad // DISPATCH_TILE) * DISPATCH_TILE
    j = jnp.arange(PAD_TILE - 1, dtype=jnp.int32)[None, :]
    trash = m_rows + jnp.arange(n_pad_entries, dtype=jnp.int32)
    pad_dest = jnp.where(j < (pcounts - counts)[:, None], (poff + counts)[:, None] + j,
                         trash[:n_pad].reshape(n_experts, PAD_TILE - 1)).reshape(n_pad)
    dest_all = jnp.concatenate([dest, pad_dest, trash[n_pad:]]).astype(jnp.int32)
    m_total = m_rows + n_pad_entries

    sb_rows = SB_ROWS
    s_max = n_experts + m_rows // sb_rows
    nsb = (pcounts + sb_rows - 1) // sb_rows
    sb_end = jnp.cumsum(nsb)
    n_used = sb_end[-1]
    s = jnp.arange(s_max, dtype=jnp.int32)
    s_eff = jnp.minimum(s, n_used - 1)
    e_of_s = jnp.sum((s_eff[:, None] >= sb_end[None, :]).astype(jnp.int32), axis=1)
    e_of_s = jnp.minimum(e_of_s, n_experts - 1)
    oh_s = (e_of_s[:, None] == experts[None, :]).astype(jnp.int32)
    local = s_eff - jnp.sum(oh_s * (sb_end - nsb)[None, :], axis=1)
    row0 = jnp.sum(oh_s * poff[None, :], axis=1) + local * sb_rows
    rows = jnp.clip(jnp.sum(oh_s * pcounts[None, :], axis=1) - local * sb_rows, 0, sb_rows)
    rows = jnp.where(s < n_used, rows, 0)
    return dict(dest=dest.astype(jnp.int32), dest_all=dest_all, m_total=m_total,
                sb_e=e_of_s.astype(jnp.int32), sb_row0=row0.astype(jnp.int32),
                sb_rows=rows.astype(jnp.int32), n_used=n_used.reshape(1).astype(jnp.int32),
                s_max=s_max, n_main=a)


def _dispatch_kernel(dest_ref, h_ref, xs_ref, z_scr, sem, *, n_main_steps):
    i = pl.program_id(0)
    base = i * DISPATCH_TILE

    def row_copy(src_ref, src_row, a):
        return pltpu.make_async_copy(src_ref.at[pl.ds(src_row, 1)],
                                     xs_ref.at[pl.ds(dest_ref[base + a], 1)], sem)

    @pl.when(i == 0)
    def _():
        z_scr[...] = jnp.zeros_like(z_scr)

    @pl.when(i < n_main_steps)
    def _():
        def issue(t, c):
            for k in range(TOP_K):
                row_copy(h_ref, t, t * TOP_K + k).start()
            return c
        lax.fori_loop(0, DISPATCH_TILE // TOP_K, issue, 0, unroll=ISSUE_UNROLL)

    @pl.when(i >= n_main_steps)
    def _():
        def issue(a, c):
            row_copy(z_scr, 0, a).start()
            return c
        lax.fori_loop(0, DISPATCH_TILE, issue, 0, unroll=4 * ISSUE_UNROLL)

    all_rows = xs_ref.at[pl.ds(0, DISPATCH_TILE)]
    pltpu.make_async_copy(all_rows, all_rows, sem).wait()


def _dispatch(h, tabs):
    n, d = h.shape
    n_main = tabs["n_main"]
    assert n_main % DISPATCH_TILE == 0
    n_steps = tabs["dest_all"].shape[0] // DISPATCH_TILE
    n_main_steps = n_main // DISPATCH_TILE
    tt = DISPATCH_TILE // TOP_K
    return pl.pallas_call(
        functools.partial(_dispatch_kernel, n_main_steps=n_main_steps),
        grid_spec=pltpu.PrefetchScalarGridSpec(
            num_scalar_prefetch=1, grid=(n_steps,),
            in_specs=[pl.BlockSpec((tt, d), lambda i, dr: (jnp.minimum(i, n_main_steps - 1), 0))],
            out_specs=pl.BlockSpec(memory_space=pl.ANY),
            scratch_shapes=[pltpu.VMEM((8, d), F32), pltpu.SemaphoreType.DMA(())]),
        out_shape=jax.ShapeDtypeStruct((tabs["m_total"], d), F32),
        compiler_params=_cparams(("arbitrary",), 16),
        name="moe_dispatch",
    )(tabs["dest_all"], h)


def _expert_kernel(sbe_ref, row0_ref, rows_ref, nused_ref, xs_ref, wg_ref, wu_ref, wd_ref,
                   bg_ref, bu_ref, bd_ref, ys_ref, x_scr, h_scr, y_scr, x_sem, y_sem):
    del sbe_ref, nused_ref
    s = pl.program_id(0)
    j = pl.program_id(1)
    rows = rows_ref[s]
    row0 = pl.multiple_of(row0_ref[s], PAD_TILE)
    big = CHUNKS[0]
    n_big = rows // big

    def for_chunks(fn):
        def body(c, carry):
            fn(pl.multiple_of(c * big, big), big)
            return carry
        lax.fori_loop(0, n_big, body, 0)
        off = n_big * big
        rem = rows - off
        for tm in CHUNKS[1:]:
            has = (rem & tm) != 0

            @pl.when(has)
            def _(off=off, tm=tm):
                fn(pl.multiple_of(off, tm), tm)
            off = off + jnp.where(has, tm, 0)

    def x_piece(p, base=row0):
        r = pl.multiple_of(p * PAD_TILE, PAD_TILE)
        return pltpu.make_async_copy(xs_ref.at[pl.ds(pl.multiple_of(base + r, PAD_TILE), PAD_TILE)],
                                     x_scr.at[pl.ds(r, PAD_TILE)], x_sem.at[p])

    def start_x(base, n_rows):
        def start(p, carry):
            x_piece(p, base).start()
            return carry
        lax.fori_loop(0, n_rows // PAD_TILE, start, 0)

    def y_copy(n, slot, r0, tm):
        return pltpu.make_async_copy(
            y_scr.at[slot, pl.ds(r0, tm)],
            ys_ref.at[pl.ds(pl.multiple_of(row0 + r0, PAD_TILE), tm), pl.ds(n * FF_TILE, FF_TILE)],
            y_sem.at[slot])

    @pl.when(rows > 0)
    def _():
        @pl.when((j == 0) & (s == 0))
        def _():
            start_x(row0, rows)

        @pl.when(j == N_FF_TILES)
        def _():
            n_sb = pl.num_programs(0)
            s_next = jnp.minimum(s + 1, n_sb - 1)
            rows_next = jnp.where(s + 1 < n_sb, rows_ref[s_next], 0)
            start_x(pl.multiple_of(row0_ref[s_next], PAD_TILE), rows_next)

        @pl.when(j < N_FF_TILES)
        def _():
            def gate_up(r0, tm):
                @pl.when(j == 0)
                def _():
                    for q in range(tm // PAD_TILE):
                        x_piece(r0 // PAD_TILE + q).wait()
                x = x_scr[pl.ds(r0, tm), :]
                hg = jnp.dot(x, wg_ref[0], preferred_element_type=F32) + bg_ref[0]
                hu = jnp.dot(x, wu_ref[0], preferred_element_type=F32) + bu_ref[0]
                hg = jnp.minimum(hg, SWIGLU_LIMIT)
                hu = jnp.clip(hu, -SWIGLU_LIMIT, SWIGLU_LIMIT)
                act = (hu + 1.0) * hg * jax.nn.sigmoid(SWIGLU_ALPHA * hg)
                h_scr[j, pl.ds(r0, tm), :] = act.astype(BF16)
            for_chunks(gate_up)

        @pl.when(j >= N_FF_TILES)
        def _():
            slot = (j - N_FF_TILES) % 2

            def down(r0, tm):
                acc = bd_ref[0] + jnp.zeros((tm, FF_TILE), F32)
                for f in range(N_FF_TILES):
                    w = wd_ref[0, f * FF_TILE:(f + 1) * FF_TILE, :].astype(BF16)
                    acc = acc + jnp.dot(h_scr[f, pl.ds(r0, tm), :], w, preferred_element_type=F32)
                y_scr[slot, pl.ds(r0, tm), :] = acc
                for n in range(N_FF_TILES):
                    @pl.when(j == N_FF_TILES + n)
                    def _(n=n):
                        y_copy(n, n % 2, r0, tm).start()
            for_chunks(down)

            @pl.when(j > N_FF_TILES)
            def _():
                for_chunks(lambda r0, tm: y_copy(0, 1 - slot, r0, tm).wait())

            @pl.when(j == 2 * N_FF_TILES - 1)
            def _():
                for_chunks(lambda r0, tm: y_copy(0, slot, r0, tm).wait())


def _experts(xs, tabs, layer, w_gate, b_gate, w_up, b_up, w_down, b_down):
    n_l, n_e, d, f = w_gate.shape
    assert f == N_FF_TILES * FF_TILE and d == N_FF_TILES * FF_TILE
    w_gate, w_up, w_down = (w.reshape(n_l * n_e, *w.shape[2:]) for w in (w_gate, w_up, w_down))
    b_gate, b_up, b_down = (b.reshape(n_l * n_e, 1, b.shape[2]) for b in (b_gate, b_up, b_down))
    m_total = xs.shape[0]
    sb_rows = SB_ROWS
    last = N_FF_TILES - 1

    def up_map(s, j, sbe, r0, nrows, nu):
        return (sbe[s], 0, jnp.where(s < nu[0], jnp.minimum(j, last), last))

    def down_map(s, j, sbe, r0, nrows, nu):
        used = s < nu[0]
        in_down = j >= N_FF_TILES
        hold_prev = used & jnp.logical_not(in_down) & (s > 0)
        expert = jnp.where(hold_prev, sbe[jnp.maximum(s - 1, 0)], sbe[s])
        tile = jnp.where(in_down, j - N_FF_TILES, jnp.where(s > 0, last, 0))
        return (expert, 0, jnp.where(used, tile, last))

    return pl.pallas_call(
        _expert_kernel,
        grid_spec=pltpu.PrefetchScalarGridSpec(
            num_scalar_prefetch=4, grid=(tabs["s_max"], 2 * N_FF_TILES),
            in_specs=[pl.BlockSpec(memory_space=pl.ANY),
                      pl.BlockSpec((1, d, FF_TILE), up_map),
                      pl.BlockSpec((1, d, FF_TILE), up_map),
                      pl.BlockSpec((1, f, FF_TILE), down_map),
                      pl.BlockSpec((1, 1, FF_TILE), up_map),
                      pl.BlockSpec((1, 1, FF_TILE), up_map),
                      pl.BlockSpec((1, 1, FF_TILE), down_map)],
            out_specs=pl.BlockSpec(memory_space=pl.ANY),
            scratch_shapes=[pltpu.VMEM((sb_rows, d), F32),
                            pltpu.VMEM((N_FF_TILES, sb_rows, FF_TILE), BF16),
                            pltpu.VMEM((2, sb_rows, FF_TILE), F32),
                            pltpu.SemaphoreType.DMA((SB_ROWS // PAD_TILE,)),
                            pltpu.SemaphoreType.DMA((2,))]),
        out_shape=jax.ShapeDtypeStruct((m_total, d), F32),
        compiler_params=_cparams(("arbitrary", "arbitrary"), 56),
        name="moe_experts",
    )(tabs["sb_e"] + layer * n_e, tabs["sb_row0"], tabs["sb_rows"], tabs["n_used"], xs,
      w_gate, w_up, w_down, b_gate, b_up, b_down)


def _combine_kernel(dest_ref, ys_ref, h_ref, gate_ref, g_ref, b_ref, o_ref, buf, sem, *, alpha):
    i = pl.program_id(0)
    n_steps = pl.num_programs(0)
    tt = COMBINE_TOKENS

    def row_copy(step, slot, t, k):
        return pltpu.make_async_copy(
            ys_ref.at[pl.ds(dest_ref[(step * tt + t) * TOP_K + k], 1)],
            buf.at[slot, k, pl.ds(t, 1)], sem.at[slot])

    def issue(step, slot):
        def body(t, c):
            for k in range(TOP_K):
                row_copy(step, slot, t, k).start()
            return c
        lax.fori_loop(0, tt, body, 0, unroll=ISSUE_UNROLL)

    @pl.when(i == 0)
    def _():
        issue(0, 0)

    @pl.when(i + 1 < n_steps)
    def _():
        issue(i + 1, (i + 1) % 2)

    slot = i % 2

    pltpu.make_async_copy(buf.at[slot], buf.at[slot], sem.at[slot]).wait()

    gates = gate_ref[...]
    y = gates[:, 0:1] * buf[slot, 0]
    for k in range(1, TOP_K):
        y = y + gates[:, k:k + 1] * buf[slot, k]
    o_ref[...] = _layer_norm(alpha * h_ref[...] + y, g_ref[...], b_ref[...])


def _combine(ys, dest, h, gates, ln_g, ln_b, alpha):
    n, d = h.shape
    tt = COMBINE_TOKENS
    assert n % tt == 0
    return pl.pallas_call(
        functools.partial(_combine_kernel, alpha=alpha),
        grid_spec=pltpu.PrefetchScalarGridSpec(
            num_scalar_prefetch=1, grid=(n // tt,),
            in_specs=[pl.BlockSpec(memory_space=pl.ANY),
                      pl.BlockSpec((tt, d), lambda i, dr: (i, 0)),
                      pl.BlockSpec((tt, TOP_K), lambda i, dr: (i, 0)),
                      pl.BlockSpec((1, d), lambda i, dr: (0, 0)),
                      pl.BlockSpec((1, d), lambda i, dr: (0, 0))],
            out_specs=pl.BlockSpec((tt, d), lambda i, dr: (i, 0)),
            scratch_shapes=[pltpu.VMEM((2, TOP_K, tt, d), F32),
                            pltpu.SemaphoreType.DMA((2,))]),
        out_shape=jax.ShapeDtypeStruct((n, d), F32),
        compiler_params=_cparams(("arbitrary",), 40),
        name="moe_combine",
    )(dest, ys, h, gates, ln_g.reshape(1, d), ln_b.reshape(1, d))


def _moe_layer(h, idx, gates, ln_g, ln_b, layer, w_gate, b_gate, w_up, b_up, w_down, b_down, alpha):
    tabs = _routing_tables(idx, w_gate.shape[1])
    xs = _dispatch(h, tabs)
    ys = _experts(xs, tabs, layer, w_gate, b_gate, w_up, b_up, w_down, b_down)
    return _combine(ys, tabs["dest"], h, gates, ln_g, ln_b, alpha)


def _proj_kernel(x_ref, w_ref, lb_ref, o_ref, *, tiles_per_section):
    sec = pl.program_id(1) // tiles_per_section
    lb = lb_ref[...]
    sub = x_ref.shape[0] // PROJ_SPLIT
    for c in range(PROJ_SPLIT):
        rows = slice(c * sub, (c + 1) * sub)
        p = jnp.dot(x_ref[rows, :], w_ref[...], preferred_element_type=F32)
        sig = jax.nn.sigmoid(p)
        forget = lb + (1.0 - lb) * sig
        o_ref[rows, :] = jnp.where(sec == 1, forget, jnp.where(sec == 2, p, p * sig))


def _hgrn_proj(h, w_in, lb):
    n, d = h.shape
    tn = 1024
    tm = 928
    assert n % tm == 0 and d % tn == 0 and w_in.shape[1] == 4 * d and tm % (8 * PROJ_SPLIT) == 0
    tps = d // tn
    return pl.pallas_call(
        functools.partial(_proj_kernel, tiles_per_section=tps),
        grid=(n // tm, 4 * tps),
        in_specs=[pl.BlockSpec((tm, d), lambda i, j: (i, 0)),
                  pl.BlockSpec((d, tn), lambda i, j: (0, j)),
                  pl.BlockSpec((1, tn), lambda i, j: (0, j % tps))],
        out_specs=pl.BlockSpec((tm, tn), lambda i, j: (i, j)),
        out_shape=jax.ShapeDtypeStruct((n, 4 * d), F32),
        compiler_params=_cparams(("arbitrary", "arbitrary"), 48),
        name="hgrn_proj",
    )(h, w_in, lb.reshape(1, d))


_NT = (((1,), (1,)), ((), ()))
_TN = (((0,), (0,)), ((), ()))


def _hgrn_block(q, f, v, st):
    c = q.shape[0]
    n_sub = c // SUB
    lf = jnp.log(f)
    k = 1.0 - f
    row = lax.broadcasted_iota(jnp.int32, (c, 1), 0)
    loc = row % SUB
    bl = lf
    for sh in (1, 2, 4, 8):
        bl = bl + jnp.where(loc >= sh, pltpu.roll(bl, sh, axis=0), 0.0)
    tot = [bl[SUB * i + SUB - 1:SUB * i + SUB, :] for i in range(n_sub)]
    off = [jnp.zeros_like(tot[0])]
    for i in range(1, n_sub):
        off.append(off[-1] + tot[i - 1])
    b_last = off[-1] + tot[-1]
    if n_sub > 1:
        blk = row // SUB
        offs = off[0]
        tots = tot[0]
        for i in range(1, n_sub):
            offs = jnp.where(blk == i, off[i], offs)
            tots = jnp.where(blk == i, tot[i], tots)
        b = bl + offs
    else:
        b = bl
        tots = tot[0]
    r2 = lax.broadcasted_iota(jnp.int32, (c, c), 0)
    c2 = lax.broadcasted_iota(jnp.int32, (c, c), 1)
    a_loc = lax.dot_general(q * jnp.exp(bl), k * jnp.exp(-bl), _NT, preferred_element_type=F32)
    a = jnp.where((r2 // SUB == c2 // SUB) & (c2 <= r2), a_loc, 0.0)
    if n_sub > 1:
        k_end = k * jnp.exp(tots - bl)
        for jb in range(n_sub - 1):
            gamma = off[jb] + tot[jb]
            q_rel = q * jnp.exp(jnp.minimum(b - gamma, 0.0))
            a_j = lax.dot_general(q_rel, k_end, _NT, preferred_element_type=F32)
            a = jnp.where((c2 // SUB == jb) & (r2 // SUB > jb), a_j, a)
    o = jnp.dot(a, v, preferred_element_type=F32)
    o = o + lax.dot_general(q * jnp.exp(b), st, _NT, preferred_element_type=F32)
    kd = k * jnp.exp(b_last - b)
    st_new = st * jnp.exp(b_last) + lax.dot_general(v, kd, _TN, preferred_element_type=F32)
    o = o * lax.rsqrt(jnp.mean(o * o, axis=-1, keepdims=True) + RMS_EPS)
    return o, st_new


def _rec_prompt_kernel(q_ref, f_ref, v_ref, o_ref, s_ref, st_scr, *, n_blocks):
    st_scr[...] = jnp.zeros_like(st_scr)

    def run(r0, c):
        rows = pl.ds(r0, c)
        for hd in range(REC_HEADS):
            cols = slice(hd * HG_DK, (hd + 1) * HG_DK)
            o, st = _hgrn_block(q_ref[rows, cols], f_ref[rows, cols], v_ref[rows, cols], st_scr[hd])
            o_ref[rows, cols] = o
            st_scr[hd] = st

    run(0, SUB)

    def body(i, carry):
        run(pl.multiple_of(SUB + i * REC_BLOCK, SUB), REC_BLOCK)
        return carry
    lax.fori_loop(0, n_blocks, body, 0, unroll=2)
    for hd in range(REC_HEADS):
        s_ref[0, hd] = st_scr[hd].T


def _rec_sample_kernel(q_ref, f_ref, v_ref, s0_ref, o_ref, s_ref, *, t_new):
    sb = s0_ref.shape[1]

    def body(i, carry):
        for u in range(REC_SEQS):
            sq = i * REC_SEQS + u
            rows = pl.ds(pl.multiple_of(sq * t_new, t_new), t_new)
            o, st = _hgrn_block_short(q_ref[rows, :], f_ref[rows, :], v_ref[rows, :],
                                      s0_ref[0, sq, 0])
            o_ref[rows, :] = o
            s_ref[0, sq, 0] = st
        return carry
    lax.fori_loop(0, sb // REC_SEQS, body, 0)


def _hgrn_block_short(q, f, v, s0):
    c = q.shape[0]
    lf = jnp.log(f)
    k = 1.0 - f
    row = lax.broadcasted_iota(jnp.int32, (c, 1), 0)
    bl = lf
    sh = 1
    while sh < c:
        bl = bl + jnp.where(row >= sh, pltpu.roll(bl, sh, axis=0), 0.0)
        sh *= 2
    b_last = bl[c - 1:c, :]
    r2 = lax.broadcasted_iota(jnp.int32, (c, c), 0)
    c2 = lax.broadcasted_iota(jnp.int32, (c, c), 1)
    qe = q * jnp.exp(bl)
    a = lax.dot_general(qe, k * jnp.exp(-bl), _NT, preferred_element_type=F32)
    a = jnp.where(c2 <= r2, a, 0.0)
    o = jnp.dot(a, v, preferred_element_type=F32)
    o = o + jnp.dot(qe, s0, preferred_element_type=F32)
    kd = k * jnp.exp(b_last - bl)
    decay_col = jnp.transpose(jnp.broadcast_to(jnp.exp(b_last), (c, b_last.shape[1])))[:, 0:1]
    s_new = s0 * decay_col + lax.dot_general(kd, v, _TN, preferred_element_type=F32)
    o = o * lax.rsqrt(jnp.mean(o * o, axis=-1, keepdims=True) + RMS_EPS)
    return o, s_new


def _hgrn_recurrence(proj, state_s, b_p, t_p, b_s, t_s):
    n, d4 = proj.shape
    d = d4 // 4
    heads = d // HG_DK
    n_p = b_p * t_p
    assert (t_p - SUB) % REC_BLOCK == 0 and t_s < SUB and t_s % 8 == 0
    assert heads % REC_HEADS == 0
    hgroups = heads // REC_HEADS
    gw = REC_HEADS * HG_DK
    o_p, s_p = pl.pallas_call(
        functools.partial(_rec_prompt_kernel, n_blocks=(t_p - SUB) // REC_BLOCK),
        grid=(b_p, hgroups),
        in_specs=[pl.BlockSpec((t_p, gw), lambda b, h: (b, h)),
                  pl.BlockSpec((t_p, gw), lambda b, h: (b, hgroups + h)),
                  pl.BlockSpec((t_p, gw), lambda b, h: (b, 2 * hgroups + h))],
        out_specs=[pl.BlockSpec((t_p, gw), lambda b, h: (b, h)),
                   pl.BlockSpec((1, REC_HEADS, HG_DK, HG_DK), lambda b, h: (b, h, 0, 0))],
        out_shape=[jax.ShapeDtypeStruct((n_p, d), F32),
                   jax.ShapeDtypeStruct((b_p, heads, HG_DK, HG_DK), F32)],
        scratch_shapes=[pltpu.VMEM((REC_HEADS, HG_DK, HG_DK), F32)],
        compiler_params=_cparams(("arbitrary", "arbitrary"), 48),
        name="hgrn_rec_prompt",
    )(proj, proj, proj)

    proj_s = proj[n_p:]
    sb = 32
    assert b_s % sb == 0
    o_s, s_s = pl.pallas_call(
        functools.partial(_rec_sample_kernel, t_new=t_s),
        grid=(heads, b_s // sb),
        in_specs=[pl.BlockSpec((sb * t_s, HG_DK), lambda h, i: (i, h)),
                  pl.BlockSpec((sb * t_s, HG_DK), lambda h, i: (i, heads + h)),
                  pl.BlockSpec((sb * t_s, HG_DK), lambda h, i: (i, 2 * heads + h)),
                  pl.BlockSpec((1, sb, 1, HG_DK, HG_DK), lambda h, i: (0, i, h, 0, 0))],
        out_specs=[pl.BlockSpec((sb * t_s, HG_DK), lambda h, i: (i, h)),
                   pl.BlockSpec((1, sb, 1, HG_DK, HG_DK), lambda h, i: (0, i, h, 0, 0))],
        out_shape=[jax.ShapeDtypeStruct((b_s * t_s, d), F32),
                   jax.ShapeDtypeStruct((1, b_s, heads, HG_DK, HG_DK), F32)],
        compiler_params=_cparams(("arbitrary", "arbitrary"), 32),
        name="hgrn_rec_sample",
    )(proj_s, proj_s, proj_s, state_s)
    return jnp.concatenate([o_p, o_s], axis=0), s_p, s_s


def _hgrn_out_kernel(o_ref, sg_ref, ng_ref, w_ref, x_ref, g_ref, b_ref, rw_ref, rb_ref,
                     h_ref, idx_ref, gate_ref, *, alpha):
    sub = o_ref.shape[0] // OUT_SPLIT
    for c in range(OUT_SPLIT):
        rows = slice(c * sub, (c + 1) * sub)
        z = (o_ref[rows, :] * ng_ref[...] * sg_ref[rows, :]).astype(BF16)
        y = jnp.dot(z, w_ref[...], preferred_element_type=F32)
        _mix_post(x_ref[rows, :], y, alpha, g_ref, b_ref, rw_ref, rb_ref, h_ref, idx_ref, gate_ref, rows)


def _hgrn_out(o, proj, norm_g, w_out, x, ln_g, ln_b, router_w, router_b, alpha):
    n, d = x.shape
    n_e = router_w.shape[1]
    tm = 464
    assert n % tm == 0 and tm % (8 * OUT_SPLIT) == 0
    consts = (ln_g.reshape(1, d), ln_b.reshape(1, d), router_w, router_b.reshape(1, n_e))
    row = lambda i: (i, 0)
    return pl.pallas_call(
        functools.partial(_hgrn_out_kernel, alpha=alpha),
        grid=(n // tm,),
        in_specs=[pl.BlockSpec((tm, d), row),
                  pl.BlockSpec((tm, d), lambda i: (i, 3)),
                  _const_spec((1, d)),
                  _const_spec((d, d)),
                  pl.BlockSpec((tm, d), row)] + [_const_spec(c.shape) for c in consts],
        out_specs=[pl.BlockSpec((tm, d), row), pl.BlockSpec((tm, TOP_K), row),
                   pl.BlockSpec((tm, TOP_K), row)],
        out_shape=[jax.ShapeDtypeStruct((n, d), F32),
                   jax.ShapeDtypeStruct((n, TOP_K), jnp.int32),
                   jax.ShapeDtypeStruct((n, TOP_K), F32)],
        compiler_params=_cparams(("arbitrary",), 56),
        name="hgrn_out",
    )(o, proj, norm_g.reshape(1, d), w_out.astype(BF16), x, *consts)


def kernel(x_prompt, x_sample, state_pool, state_hgrn, meta_tokens, pool_w, pool_scale, hg_w_in, hg_lb, hg_norm_g, hg_w_out, ln_g, ln_b, router_w, router_b, w_gate, b_gate, w_up, b_up, w_down, b_down):
    depth = ln_g.shape[0]
    assert depth == 2, "layer 0 = pooling mixer, layer 1 = HGRN2 mixer"
    alpha = float((2 * depth) ** 0.25)
    b_p, seq, d = x_prompt.shape
    b_s, t_s, _ = x_sample.shape
    t_p = seq + N_META
    n_p = b_p * t_p

    xp = jnp.concatenate([jnp.broadcast_to(meta_tokens[None], (b_p, N_META, d)), x_prompt], axis=1)
    prefix = state_pool[0]
    xs_ext = jnp.concatenate([jnp.zeros((b_s, 16 - POOL_BUF, d), F32), prefix, x_sample], axis=1)

    h, idx, gates = _pool_layer(xp, xs_ext, pool_w[0], pool_scale[0], ln_g[0, 0], ln_b[0, 0],
                                router_w[0], router_b[0], alpha)
    moe_w = (w_gate, b_gate, w_up, b_up, w_down, b_down)
    h = _moe_layer(h, idx, gates, ln_g[0, 1], ln_b[0, 1], 0, *moe_w, alpha)

    sm = jax.nn.softmax(hg_lb.astype(F32), axis=0)
    lb = (jnp.cumsum(sm, axis=0) - sm[0:1])[1]
    proj = _hgrn_proj(h, hg_w_in[0], lb)
    o, s_p, s_s = _hgrn_recurrence(proj, state_hgrn, b_p, t_p, b_s, t_s)
    h, idx, gates = _hgrn_out(o, proj, hg_norm_g[0], hg_w_out[0], h, ln_g[1, 0], ln_b[1, 0],
                              router_w[1], router_b[1], alpha)
    h = _moe_layer(h, idx, gates, ln_g[1, 1], ln_b[1, 1], 1, *moe_w, alpha)

    y_prompt = h[:n_p].reshape(b_p, t_p, d)[:, N_META:]
    y_sample = h[n_p:].reshape(b_s, t_s, d)
    pool_p = x_prompt[:, seq - POOL_BUF:][None]
    pool_s = jnp.concatenate([prefix, x_sample], axis=1)[:, t_s:][None]
    return (y_prompt, y_sample, pool_p, pool_s, s_p[None], s_s)
```

```python
import functools

import jax
import jax.numpy as jnp
from jax import lax
from jax.experimental import pallas as pl
from jax.experimental.pallas import tpu as pltpu

N_META = 16
POOL_WINDOWS = (2, 4, 8, 16)
POOL_BUF = max(POOL_WINDOWS) - 1
PAST_LEN = 16384
HG_DK = 128
SUB = 16
TOP_K = 4
SWIGLU_LIMIT = 7.0
SWIGLU_ALPHA = 1.702
LN_EPS = 1e-5
RMS_EPS = 1e-6

PAD_TILE = 128
CHUNKS = (512, 256, 128)
SB_ROWS = 1536
FF_TILE = 512
N_FF_TILES = 4
REC_BLOCK = 64
REC_HEADS = 4
REC_SEQS = 8
DISPATCH_TILE = 640
COMBINE_TOKENS = 160
ISSUE_UNROLL = 4
PROJ_SPLIT = 2
OUT_SPLIT = 2
MIB = 1024 * 1024

F32 = jnp.float32
BF16 = jnp.bfloat16


def _cparams(sem, vmem_mib):
    return pltpu.CompilerParams(dimension_semantics=sem, vmem_limit_bytes=vmem_mib * MIB)


def _layer_norm(x, g, b):
    mu = jnp.mean(x, axis=-1, keepdims=True)
    xc = x - mu
    var = jnp.mean(xc * xc, axis=-1, keepdims=True)
    return xc * lax.rsqrt(var + LN_EPS) * g + b


def _dot_3pass(a, b):
    a_hi = a.astype(BF16)
    a_lo = (a - a_hi.astype(F32)).astype(BF16)
    b_hi = b.astype(BF16)
    b_lo = (b - b_hi.astype(F32)).astype(BF16)

    def d(x, y):
        return jnp.dot(x, y, preferred_element_type=F32)
    return d(a_hi, b_hi) + (d(a_hi, b_lo) + d(a_lo, b_hi))


def _route_rows(h, rw, rb):
    logits = _dot_3pass(h, rw) + rb
    n_e = logits.shape[-1]
    lane = lax.broadcasted_iota(jnp.int32, logits.shape, 1)
    vals, ids = [], []
    l = logits
    for _ in range(TOP_K):
        m = jnp.max(l, axis=-1, keepdims=True)
        ix = jnp.min(jnp.where(l == m, lane, n_e), axis=-1, keepdims=True)
        vals.append(m)
        ids.append(ix)
        l = jnp.where(lane == ix, -jnp.inf, l)
    v = jnp.concatenate(vals, axis=1)
    e = jnp.exp(v - vals[0])
    gates = e / jnp.sum(e, axis=-1, keepdims=True)
    return jnp.concatenate(ids, axis=1), gates


def _mix_post(x, y, alpha, g_ref, b_ref, rw_ref, rb_ref, h_ref, idx_ref, gate_ref, rows=slice(None)):
    h = _layer_norm(alpha * x + y, g_ref[...], b_ref[...])
    h_ref[rows, :] = h
    ids, gates = _route_rows(h, rw_ref[...], rb_ref[...])
    idx_ref[rows, :] = ids
    gate_ref[rows, :] = gates


def _window_sum(e, w):
    s = e
    sh = 1
    while sh < w:
        s = s + pltpu.roll(s, sh, axis=0)
        sh *= 2
    return s


def _pool_prompt_kernel(x_ref, halo_ref, pw_ref, ps_ref, g_ref, b_ref, rw_ref, rb_ref,
                        h_ref, idx_ref, gate_ref, *, tt, alpha):
    i = pl.program_id(1)
    cur = x_ref[0]
    halo = jnp.where(i > 0, halo_ref[0], 0.0)
    ext = jnp.concatenate([halo, cur], axis=0)
    grp = cur.shape[1] // len(POOL_WINDOWS)
    pos = i * tt + lax.broadcasted_iota(jnp.int32, (tt, 1), 0)
    ys = []
    for gi, w in enumerate(POOL_WINDOWS):
        s = _window_sum(ext[:, gi * grp:(gi + 1) * grp], w)[16:]
        cnt = jnp.minimum(w, pos + 1).astype(F32)
        d = s / cnt - cur[:, gi * grp:(gi + 1) * grp]
        ys.append(_dot_3pass(d, pw_ref[gi]))
    y = jnp.concatenate(ys, axis=1) * ps_ref[...]
    _mix_post(cur, y, alpha, g_ref, b_ref, rw_ref, rb_ref, h_ref, idx_ref, gate_ref)


def _pool_sample_kernel(x_ref, pw_ref, ps_ref, g_ref, b_ref, rw_ref, rb_ref,
                        h_ref, idx_ref, gate_ref, *, alpha, t_new):
    sb, te, d_model = x_ref.shape
    ext = x_ref[...].reshape(sb * te, d_model)
    grp = d_model // len(POOL_WINDOWS)
    cur = x_ref[:, 16:, :].reshape(sb * t_new, d_model)
    tpos = lax.broadcasted_iota(jnp.int32, (sb, t_new, 1), 1).reshape(sb * t_new, 1)
    ys = []
    for gi, w in enumerate(POOL_WINDOWS):
        s = _window_sum(ext[:, gi * grp:(gi + 1) * grp], w)
        s = s.reshape(sb, te, grp)[:, 16:, :].reshape(sb * t_new, grp)
        cnt = jnp.minimum(w, PAST_LEN + tpos + 1).astype(F32)
        d = s / cnt - cur[:, gi * grp:(gi + 1) * grp]
        ys.append(_dot_3pass(d, pw_ref[gi]))
    y = jnp.concatenate(ys, axis=1) * ps_ref[...]
    _mix_post(cur, y, alpha, g_ref, b_ref, rw_ref, rb_ref, h_ref, idx_ref, gate_ref)


def _const_spec(shape):
    nd = len(shape)
    return pl.BlockSpec(shape, lambda *_: (0,) * nd)


def _pool_layer(xp, xs_ext, pool_w, pool_scale, ln_g, ln_b, router_w, router_b, alpha):
    b_p, t_p, d = xp.shape
    n_e = router_w.shape[1]
    tt = 688
    assert t_p % tt == 0 and tt % 16 == 0
    ps = pool_scale.reshape(1, d)
    g2, b2, rb2 = ln_g.reshape(1, d), ln_b.reshape(1, d), router_b.reshape(1, n_e)
    consts = (pool_w, ps, g2, b2, router_w, rb2)
    const_specs = [_const_spec(c.shape) for c in consts]
    n_p = b_p * t_p
    nt = t_p // tt
    hp, ip, gp = pl.pallas_call(
        functools.partial(_pool_prompt_kernel, tt=tt, alpha=alpha),
        grid=(b_p, nt),
        in_specs=[pl.BlockSpec((1, tt, d), lambda b, i: (b, i, 0)),
                  pl.BlockSpec((1, 16, d), lambda b, i: (b, jnp.maximum(i * (tt // 16) - 1, 0), 0))]
        + const_specs,
        out_specs=[pl.BlockSpec((tt, d), lambda b, i: (b * nt + i, 0)),
                   pl.BlockSpec((tt, TOP_K), lambda b, i: (b * nt + i, 0)),
                   pl.BlockSpec((tt, TOP_K), lambda b, i: (b * nt + i, 0))],
        out_shape=[jax.ShapeDtypeStruct((n_p, d), F32),
                   jax.ShapeDtypeStruct((n_p, TOP_K), jnp.int32),
                   jax.ShapeDtypeStruct((n_p, TOP_K), F32)],
        compiler_params=_cparams(("arbitrary", "arbitrary"), 56),
        name="pool_prompt",
    )(xp, xp, *consts)

    b_s, te, _ = xs_ext.shape
    t_new = te - 16
    sb = 16
    assert b_s % sb == 0
    n_s = b_s * t_new
    hs, is_, gs = pl.pallas_call(
        functools.partial(_pool_sample_kernel, alpha=alpha, t_new=t_new),
        grid=(b_s // sb,),
        in_specs=[pl.BlockSpec((sb, te, d), lambda i: (i, 0, 0))] + const_specs,
        out_specs=[pl.BlockSpec((sb * t_new, d), lambda i: (i, 0)),
                   pl.BlockSpec((sb * t_new, TOP_K), lambda i: (i, 0)),
                   pl.BlockSpec((sb * t_new, TOP_K), lambda i: (i, 0))],
        out_shape=[jax.ShapeDtypeStruct((n_s, d), F32),
                   jax.ShapeDtypeStruct((n_s, TOP_K), jnp.int32),
                   jax.ShapeDtypeStruct((n_s, TOP_K), F32)],
        compiler_params=_cparams(("arbitrary",), 40),
        name="pool_sample",
    )(xs_ext, *consts)
    return (jnp.concatenate([hp, hs], axis=0), jnp.concatenate([ip, is_], axis=0),
            jnp.concatenate([gp, gs], axis=0))


def _routing_tables(idx, n_experts):
    n, k = idx.shape
    a = n * k
    e_flat = idx.reshape(a)
    experts = jnp.arange(n_experts, dtype=jnp.int32)
    onehot = (e_flat[:, None] == experts[None, :]).astype(jnp.int32)
    csum = jnp.cumsum(onehot, axis=0)
    counts = csum[-1]
    pcounts = ((counts + PAD_TILE - 1) // PAD_TILE) * PAD_TILE
    pend = jnp.cumsum(pcounts)
    poff = pend - pcounts
    dest = jnp.sum(onehot * (csum - onehot + poff[None, :]), axis=1)

    m_rows = -(-(a + n_experts * (PAD_TILE - 1)) // PAD_TILE) * PAD_TILE
    n_pad = n_experts * (PAD_TILE - 1)
    n_pad_entries = -(-n_pad // DISPATCH_TILE) * DISPATCH_TILE
    j = jnp.arange(PAD_TILE - 1, dtype=jnp.int32)[None, :]
    trash = m_rows + jnp.arange(n_pad_entries, dtype=jnp.int32)
    pad_dest = jnp.where(j < (pcounts - counts)[:, None], (poff + counts)[:, None] + j,
                         trash[:n_pad].reshape(n_experts, PAD_TILE - 1)).reshape(n_pad)
    dest_all = jnp.concatenate([dest, pad_dest, trash[n_pad:]]).astype(jnp.int32)
    m_total = m_rows + n_pad_entries

    sb_rows = SB_ROWS
    s_max = n_experts + m_rows // sb_rows
    nsb = (pcounts + sb_rows - 1) // sb_rows
    sb_end = jnp.cumsum(nsb)
    n_used = sb_end[-1]
    s = jnp.arange(s_max, dtype=jnp.int32)
    s_eff = jnp.minimum(s, n_used - 1)
    e_of_s = jnp.sum((s_eff[:, None] >= sb_end[None, :]).astype(jnp.int32), axis=1)
    e_of_s = jnp.minimum(e_of_s, n_experts - 1)
    oh_s = (e_of_s[:, None] == experts[None, :]).astype(jnp.int32)
    local = s_eff - jnp.sum(oh_s * (sb_end - nsb)[None, :], axis=1)
    row0 = jnp.sum(oh_s * poff[None, :], axis=1) + local * sb_rows
    rows = jnp.clip(jnp.sum(oh_s * pcounts[None, :], axis=1) - local * sb_rows, 0, sb_rows)
    rows = jnp.where(s < n_used, rows, 0)
    return dict(dest=dest.astype(jnp.int32), dest_all=dest_all, m_total=m_total,
                sb_e=e_of_s.astype(jnp.int32), sb_row0=row0.astype(jnp.int32),
                sb_rows=rows.astype(jnp.int32), n_used=n_used.reshape(1).astype(jnp.int32),
                s_max=s_max, n_main=a)


def _dispatch_kernel(dest_ref, h_ref, xs_ref, z_scr, sem, *, n_main_steps):
    i = pl.program_id(0)
    base = i * DISPATCH_TILE

    def row_copy(src_ref, src_row, a):
        return pltpu.make_async_copy(src_ref.at[pl.ds(src_row, 1)],
                                     xs_ref.at[pl.ds(dest_ref[base + a], 1)], sem)

    @pl.when(i == 0)
    def _():
        z_scr[...] = jnp.zeros_like(z_scr)

    @pl.when(i < n_main_steps)
    def _():
        def issue(t, c):
            for k in range(TOP_K):
                row_copy(h_ref, t, t * TOP_K + k).start()
            return c
        lax.fori_loop(0, DISPATCH_TILE // TOP_K, issue, 0, unroll=ISSUE_UNROLL)

    @pl.when(i >= n_main_steps)
    def _():
        def issue(a, c):
            row_copy(z_scr, 0, a).start()
            return c
        lax.fori_loop(0, DISPATCH_TILE, issue, 0, unroll=4 * ISSUE_UNROLL)

    all_rows = xs_ref.at[pl.ds(0, DISPATCH_TILE)]
    pltpu.make_async_copy(all_rows, all_rows, sem).wait()


def _dispatch(h, tabs):
    n, d = h.shape
    n_main = tabs["n_main"]
    assert n_main % DISPATCH_TILE == 0
    n_steps = tabs["dest_all"].shape[0] // DISPATCH_TILE
    n_main_steps = n_main // DISPATCH_TILE
    tt = DISPATCH_TILE // TOP_K
    return pl.pallas_call(
        functools.partial(_dispatch_kernel, n_main_steps=n_main_steps),
        grid_spec=pltpu.PrefetchScalarGridSpec(
            num_scalar_prefetch=1, grid=(n_steps,),
            in_specs=[pl.BlockSpec((tt, d), lambda i, dr: (jnp.minimum(i, n_main_steps - 1), 0))],
            out_specs=pl.BlockSpec(memory_space=pl.ANY),
            scratch_shapes=[pltpu.VMEM((8, d), F32), pltpu.SemaphoreType.DMA(())]),
        out_shape=jax.ShapeDtypeStruct((tabs["m_total"], d), F32),
        compiler_params=_cparams(("arbitrary",), 16),
        name="moe_dispatch",
    )(tabs["dest_all"], h)


def _expert_kernel(sbe_ref, row0_ref, rows_ref, nused_ref, xs_ref, wg_ref, wu_ref, wd_ref,
                   bg_ref, bu_ref, bd_ref, ys_ref, x_scr, h_scr, y_scr, x_sem, y_sem):
    del sbe_ref
    s = pl.program_id(0)
    j = pl.program_id(1)
    n_sb = pl.num_programs(0)
    rows = rows_ref[s]
    row0 = pl.multiple_of(row0_ref[s], PAD_TILE)
    rows_prev = rows_ref[jnp.maximum(s - 1, 0)]
    s_next = jnp.minimum(s + 1, n_sb - 1)
    rows_next = jnp.where(s + 1 < n_sb, rows_ref[s_next], 0)
    row0_next = pl.multiple_of(row0_ref[s_next], PAD_TILE)
    is_last = s == nused_ref[0] - 1
    big = CHUNKS[0]

    def for_chunks(fn, n_rows=rows):
        n_big = n_rows // big

        def body(c, carry):
            fn(pl.multiple_of(c * big, big), big)
            return carry
        lax.fori_loop(0, n_big, body, 0)
        off = n_big * big
        rem = n_rows - off
        for tm in CHUNKS[1:]:
            has = (rem & tm) != 0

            @pl.when(has)
            def _(off=off, tm=tm):
                fn(pl.multiple_of(off, tm), tm)
            off = off + jnp.where(has, tm, 0)

    def x_piece(p, base=row0):
        r = pl.multiple_of(p * PAD_TILE, PAD_TILE)
        return pltpu.make_async_copy(xs_ref.at[pl.ds(pl.multiple_of(base + r, PAD_TILE), PAD_TILE)],
                                     x_scr.at[pl.ds(r, PAD_TILE)], x_sem.at[p])

    def start_x(base, first_piece, n_rows):
        def start(p, carry):
            x_piece(p, base).start()
            return carry
        lax.fori_loop(first_piece, n_rows // PAD_TILE, start, 0)

    def y_copy(n, slot, r0, tm):
        return pltpu.make_async_copy(
            y_scr.at[slot, pl.ds(r0, tm)],
            ys_ref.at[pl.ds(pl.multiple_of(row0 + r0, PAD_TILE), tm), pl.ds(n * FF_TILE, FF_TILE)],
            y_sem.at[slot])

    @pl.when(rows > 0)
    def _():
        @pl.when((j == 0) & (s == 0))
        def _():
            start_x(row0, 0, rows)

        @pl.when(j == N_FF_TILES)
        def _():
            start_x(row0_next, rows // PAD_TILE, rows_next)

        @pl.when(j < N_FF_TILES)
        def _():
            def gate_up(r0, tm):
                @pl.when(j == 0)
                def _():
                    for q in range(tm // PAD_TILE):
                        x_piece(r0 // PAD_TILE + q).wait()
                x = x_scr[pl.ds(r0, tm), :]
                hg = jnp.dot(x, wg_ref[0], preferred_element_type=F32) + bg_ref[0]
                hu = jnp.dot(x, wu_ref[0], preferred_element_type=F32) + bu_ref[0]
                hg = jnp.minimum(hg, SWIGLU_LIMIT)
                hu = jnp.clip(hu, -SWIGLU_LIMIT, SWIGLU_LIMIT)
                act = (hu + 1.0) * hg * jax.nn.sigmoid(SWIGLU_ALPHA * hg)
                h_scr[j, pl.ds(r0, tm), :] = act.astype(BF16)

                @pl.when(j == N_FF_TILES - 1)
                def _():
                    for q in range(tm // PAD_TILE):
                        piece = r0 // PAD_TILE + q

                        @pl.when(piece < rows_next // PAD_TILE)
                        def _(piece=piece):
                            x_piece(piece, row0_next).start()
            for_chunks(gate_up)

        @pl.when(j >= N_FF_TILES)
        def _():
            slot = (j - N_FF_TILES) % 2

            def down(r0, tm):
                acc = bd_ref[0] + jnp.zeros((tm, FF_TILE), F32)
                for f in range(N_FF_TILES):
                    w = wd_ref[0, f * FF_TILE:(f + 1) * FF_TILE, :].astype(BF16)
                    acc = acc + jnp.dot(h_scr[f, pl.ds(r0, tm), :], w, preferred_element_type=F32)
                y_scr[slot, pl.ds(r0, tm), :] = acc
                for n in range(N_FF_TILES):
                    @pl.when(j == N_FF_TILES + n)
                    def _(n=n):
                        y_copy(n, n % 2, r0, tm).start()
            for_chunks(down)

            @pl.when(j > N_FF_TILES)
            def _():
                for_chunks(lambda r0, tm: y_copy(0, 1 - slot, r0, tm).wait())

            @pl.when((j == N_FF_TILES) & (s > 0))
            def _():
                for_chunks(lambda r0, tm: y_copy(0, 1 - slot, r0, tm).wait(), rows_prev)

            @pl.when((j == 2 * N_FF_TILES - 1) & is_last)
            def _():
                for_chunks(lambda r0, tm: y_copy(0, slot, r0, tm).wait())


def _experts(xs, tabs, layer, w_gate, b_gate, w_up, b_up, w_down, b_down):
    n_l, n_e, d, f = w_gate.shape
    assert f == N_FF_TILES * FF_TILE and d == N_FF_TILES * FF_TILE
    w_gate, w_up, w_down = (w.reshape(n_l * n_e, *w.shape[2:]) for w in (w_gate, w_up, w_down))
    b_gate, b_up, b_down = (b.reshape(n_l * n_e, 1, b.shape[2]) for b in (b_gate, b_up, b_down))
    m_total = xs.shape[0]
    sb_rows = SB_ROWS
    last = N_FF_TILES - 1

    def up_map(s, j, sbe, r0, nrows, nu):
        return (sbe[s], 0, jnp.where(s < nu[0], jnp.minimum(j, last), last))

    def down_map(s, j, sbe, r0, nrows, nu):
        used = s < nu[0]
        in_down = j >= N_FF_TILES
        hold_prev = used & jnp.logical_not(in_down) & (s > 0)
        expert = jnp.where(hold_prev, sbe[jnp.maximum(s - 1, 0)], sbe[s])
        tile = jnp.where(in_down, j - N_FF_TILES, jnp.where(s > 0, last, 0))
        return (expert, 0, jnp.where(used, tile, last))

    return pl.pallas_call(
        _expert_kernel,
        grid_spec=pltpu.PrefetchScalarGridSpec(
            num_scalar_prefetch=4, grid=(tabs["s_max"], 2 * N_FF_TILES),
            in_specs=[pl.BlockSpec(memory_space=pl.ANY),
                      pl.BlockSpec((1, d, FF_TILE), up_map),
                      pl.BlockSpec((1, d, FF_TILE), up_map),
                      pl.BlockSpec((1, f, FF_TILE), down_map),
                      pl.BlockSpec((1, 1, FF_TILE), up_map),
                      pl.BlockSpec((1, 1, FF_TILE), up_map),
                      pl.BlockSpec((1, 1, FF_TILE), down_map)],
            out_specs=pl.BlockSpec(memory_space=pl.ANY),
            scratch_shapes=[pltpu.VMEM((sb_rows, d), F32),
                            pltpu.VMEM((N_FF_TILES, sb_rows, FF_TILE), BF16),
                            pltpu.VMEM((2, sb_rows, FF_TILE), F32),
                            pltpu.SemaphoreType.DMA((SB_ROWS // PAD_TILE,)),
                            pltpu.SemaphoreType.DMA((2,))]),
        out_shape=jax.ShapeDtypeStruct((m_total, d), F32),
        compiler_params=_cparams(("arbitrary", "arbitrary"), 56),
        name="moe_experts",
    )(tabs["sb_e"] + layer * n_e, tabs["sb_row0"], tabs["sb_rows"], tabs["n_used"], xs,
      w_gate, w_up, w_down, b_gate, b_up, b_down)


def _combine_kernel(dest_ref, ys_ref, h_ref, gate_ref, g_ref, b_ref, o_ref, buf, sem, *, alpha):
    i = pl.program_id(0)
    n_steps = pl.num_programs(0)
    tt = COMBINE_TOKENS

    def row_copy(step, slot, t, k):
        return pltpu.make_async_copy(
            ys_ref.at[pl.ds(dest_ref[(step * tt + t) * TOP_K + k], 1)],
            buf.at[slot, k, pl.ds(t, 1)], sem.at[slot])

    def issue(step, slot):
        def body(t, c):
            for k in range(TOP_K):
                row_copy(step, slot, t, k).start()
            return c
        lax.fori_loop(0, tt, body, 0, unroll=ISSUE_UNROLL)

    @pl.when(i == 0)
    def _():
        issue(0, 0)

    @pl.when(i + 1 < n_steps)
    def _():
        issue(i + 1, (i + 1) % 2)

    slot = i % 2

    pltpu.make_async_copy(buf.at[slot], buf.at[slot], sem.at[slot]).wait()

    gates = gate_ref[...]
    y = gates[:, 0:1] * buf[slot, 0]
    for k in range(1, TOP_K):
        y = y + gates[:, k:k + 1] * buf[slot, k]
    o_ref[...] = _layer_norm(alpha * h_ref[...] + y, g_ref[...], b_ref[...])


def _combine(ys, dest, h, gates, ln_g, ln_b, alpha):
    n, d = h.shape
    tt = COMBINE_TOKENS
    assert n % tt == 0
    return pl.pallas_call(
        functools.partial(_combine_kernel, alpha=alpha),
        grid_spec=pltpu.PrefetchScalarGridSpec(
            num_scalar_prefetch=1, grid=(n // tt,),
            in_specs=[pl.BlockSpec(memory_space=pl.ANY),
                      pl.BlockSpec((tt, d), lambda i, dr: (i, 0)),
                      pl.BlockSpec((tt, TOP_K), lambda i, dr: (i, 0)),
                      pl.BlockSpec((1, d), lambda i, dr: (0, 0)),
                      pl.BlockSpec((1, d), lambda i, dr: (0, 0))],
            out_specs=pl.BlockSpec((tt, d), lambda i, dr: (i, 0)),
            scratch_shapes=[pltpu.VMEM((2, TOP_K, tt, d), F32),
                            pltpu.SemaphoreType.DMA((2,))]),
        out_shape=jax.ShapeDtypeStruct((n, d), F32),
        compiler_params=_cparams(("arbitrary",), 40),
        name="moe_combine",
    )(dest, ys, h, gates, ln_g.reshape(1, d), ln_b.reshape(1, d))


def _moe_layer(h, idx, gates, ln_g, ln_b, layer, w_gate, b_gate, w_up, b_up, w_down, b_down, alpha):
    tabs = _routing_tables(idx, w_gate.shape[1])
    xs = _dispatch(h, tabs)
    ys = _experts(xs, tabs, layer, w_gate, b_gate, w_up, b_up, w_down, b_down)
    return _combine(ys, tabs["dest"], h, gates, ln_g, ln_b, alpha)


def _proj_kernel(x_ref, w_ref, lb_ref, o_ref, *, tiles_per_section):
    sec = pl.program_id(1) // tiles_per_section
    lb = lb_ref[...]
    sub = x_ref.shape[0] // PROJ_SPLIT
    for c in range(PROJ_SPLIT):
        rows = slice(c * sub, (c + 1) * sub)
        p = jnp.dot(x_ref[rows, :], w_ref[...], preferred_element_type=F32)
        sig = jax.nn.sigmoid(p)
        forget = lb + (1.0 - lb) * sig
        o_ref[rows, :] = jnp.where(sec == 1, forget, jnp.where(sec == 2, p, p * sig))


def _hgrn_proj(h, w_in, lb):
    n, d = h.shape
    tn = 1024
    tm = 928
    assert n % tm == 0 and d % tn == 0 and w_in.shape[1] == 4 * d and tm % (8 * PROJ_SPLIT) == 0
    tps = d // tn
    return pl.pallas_call(
        functools.partial(_proj_kernel, tiles_per_section=tps),
        grid=(n // tm, 4 * tps),
        in_specs=[pl.BlockSpec((tm, d), lambda i, j: (i, 0)),
                  pl.BlockSpec((d, tn), lambda i, j: (0, j)),
                  pl.BlockSpec((1, tn), lambda i, j: (0, j % tps))],
        out_specs=pl.BlockSpec((tm, tn), lambda i, j: (i, j)),
        out_shape=jax.ShapeDtypeStruct((n, 4 * d), F32),
        compiler_params=_cparams(("arbitrary", "arbitrary"), 48),
        name="hgrn_proj",
    )(h, w_in, lb.reshape(1, d))


_NT = (((1,), (1,)), ((), ()))
_TN = (((0,), (0,)), ((), ()))


def _hgrn_block(q, f, v, st):
    c = q.shape[0]
    n_sub = c // SUB
    lf = jnp.log(f)
    k = 1.0 - f
    row = lax.broadcasted_iota(jnp.int32, (c, 1), 0)
    loc = row % SUB
    bl = lf
    for sh in (1, 2, 4, 8):
        bl = bl + jnp.where(loc >= sh, pltpu.roll(bl, sh, axis=0), 0.0)
    tot = [bl[SUB * i + SUB - 1:SUB * i + SUB, :] for i in range(n_sub)]
    off = [jnp.zeros_like(tot[0])]
    for i in range(1, n_sub):
        off.append(off[-1] + tot[i - 1])
    b_last = off[-1] + tot[-1]
    if n_sub > 1:
        blk = row // SUB
        offs = off[0]
        tots = tot[0]
        for i in range(1, n_sub):
            offs = jnp.where(blk == i, off[i], offs)
            tots = jnp.where(blk == i, tot[i], tots)
        b = bl + offs
    else:
        b = bl
        tots = tot[0]
    r2 = lax.broadcasted_iota(jnp.int32, (c, c), 0)
    c2 = lax.broadcasted_iota(jnp.int32, (c, c), 1)
    a_loc = lax.dot_general(q * jnp.exp(bl), k * jnp.exp(-bl), _NT, preferred_element_type=F32)
    a = jnp.where((r2 // SUB == c2 // SUB) & (c2 <= r2), a_loc, 0.0)
    if n_sub > 1:
        k_end = k * jnp.exp(tots - bl)
        for jb in range(n_sub - 1):
            gamma = off[jb] + tot[jb]
            q_rel = q * jnp.exp(jnp.minimum(b - gamma, 0.0))
            a_j = lax.dot_general(q_rel, k_end, _NT, preferred_element_type=F32)
            a = jnp.where((c2 // SUB == jb) & (r2 // SUB > jb), a_j, a)
    o = jnp.dot(a, v, preferred_element_type=F32)
    o = o + lax.dot_general(q * jnp.exp(b), st, _NT, preferred_element_type=F32)
    kd = k * jnp.exp(b_last - b)
    st_new = st * jnp.exp(b_last) + lax.dot_general(v, kd, _TN, preferred_element_type=F32)
    o = o * lax.rsqrt(jnp.mean(o * o, axis=-1, keepdims=True) + RMS_EPS)
    return o, st_new


def _rec_prompt_kernel(q_ref, f_ref, v_ref, o_ref, s_ref, st_scr, *, n_blocks):
    st_scr[...] = jnp.zeros_like(st_scr)

    def run(r0, c):
        rows = pl.ds(r0, c)
        for hd in range(REC_HEADS):
            cols = slice(hd * HG_DK, (hd + 1) * HG_DK)
            o, st = _hgrn_block(q_ref[rows, cols], f_ref[rows, cols], v_ref[rows, cols], st_scr[hd])
            o_ref[rows, cols] = o
            st_scr[hd] = st

    run(0, SUB)

    def body(i, carry):
        run(pl.multiple_of(SUB + i * REC_BLOCK, SUB), REC_BLOCK)
        return carry
    lax.fori_loop(0, n_blocks, body, 0, unroll=2)
    for hd in range(REC_HEADS):
        s_ref[0, hd] = st_scr[hd].T


def _rec_sample_kernel(q_ref, f_ref, v_ref, s0_ref, o_ref, s_ref, *, t_new):
    sb = s0_ref.shape[1]

    def body(i, carry):
        for u in range(REC_SEQS):
            sq = i * REC_SEQS + u
            rows = pl.ds(pl.multiple_of(sq * t_new, t_new), t_new)
            o, st = _hgrn_block_short(q_ref[rows, :], f_ref[rows, :], v_ref[rows, :],
                                      s0_ref[0, sq, 0].T)
            o_ref[rows, :] = o
            s_ref[0, sq, 0] = st.T
        return carry
    lax.fori_loop(0, sb // REC_SEQS, body, 0)


def _hgrn_block_short(q, f, v, st):
    c = q.shape[0]
    lf = jnp.log(f)
    k = 1.0 - f
    row = lax.broadcasted_iota(jnp.int32, (c, 1), 0)
    bl = lf
    sh = 1
    while sh < c:
        bl = bl + jnp.where(row >= sh, pltpu.roll(bl, sh, axis=0), 0.0)
        sh *= 2
    b_last = bl[c - 1:c, :]
    r2 = lax.broadcasted_iota(jnp.int32, (c, c), 0)
    c2 = lax.broadcasted_iota(jnp.int32, (c, c), 1)
    qe = q * jnp.exp(bl)
    a = lax.dot_general(qe, k * jnp.exp(-bl), _NT, preferred_element_type=F32)
    a = jnp.where(c2 <= r2, a, 0.0)
    o = jnp.dot(a, v, preferred_element_type=F32)
    o = o + lax.dot_general(qe, st, _NT, preferred_element_type=F32)
    kd = k * jnp.exp(b_last - bl)
    st_new = st * jnp.exp(b_last) + lax.dot_general(v, kd, _TN, preferred_element_type=F32)
    o = o * lax.rsqrt(jnp.mean(o * o, axis=-1, keepdims=True) + RMS_EPS)
    return o, st_new


def _hgrn_recurrence(proj, state_s, b_p, t_p, b_s, t_s):
    n, d4 = proj.shape
    d = d4 // 4
    heads = d // HG_DK
    n_p = b_p * t_p
    assert (t_p - SUB) % REC_BLOCK == 0 and t_s < SUB and t_s % 8 == 0
    assert heads % REC_HEADS == 0
    hgroups = heads // REC_HEADS
    gw = REC_HEADS * HG_DK
    o_p, s_p = pl.pallas_call(
        functools.partial(_rec_prompt_kernel, n_blocks=(t_p - SUB) // REC_BLOCK),
        grid=(b_p, hgroups),
        in_specs=[pl.BlockSpec((t_p, gw), lambda b, h: (b, h)),
                  pl.BlockSpec((t_p, gw), lambda b, h: (b, hgroups + h)),
                  pl.BlockSpec((t_p, gw), lambda b, h: (b, 2 * hgroups + h))],
        out_specs=[pl.BlockSpec((t_p, gw), lambda b, h: (b, h)),
                   pl.BlockSpec((1, REC_HEADS, HG_DK, HG_DK), lambda b, h: (b, h, 0, 0))],
        out_shape=[jax.ShapeDtypeStruct((n_p, d), F32),
                   jax.ShapeDtypeStruct((b_p, heads, HG_DK, HG_DK), F32)],
        scratch_shapes=[pltpu.VMEM((REC_HEADS, HG_DK, HG_DK), F32)],
        compiler_params=_cparams(("arbitrary", "arbitrary"), 48),
        name="hgrn_rec_prompt",
    )(proj, proj, proj)

    proj_s = proj[n_p:]
    sb = 32
    assert b_s % sb == 0
    o_s, s_s = pl.pallas_call(
        functools.partial(_rec_sample_kernel, t_new=t_s),
        grid=(heads, b_s // sb),
        in_specs=[pl.BlockSpec((sb * t_s, HG_DK), lambda h, i: (i, h)),
                  pl.BlockSpec((sb * t_s, HG_DK), lambda h, i: (i, heads + h)),
                  pl.BlockSpec((sb * t_s, HG_DK), lambda h, i: (i, 2 * heads + h)),
                  pl.BlockSpec((1, sb, 1, HG_DK, HG_DK), lambda h, i: (0, i, h, 0, 0))],
        out_specs=[pl.BlockSpec((sb * t_s, HG_DK), lambda h, i: (i, h)),
                   pl.BlockSpec((1, sb, 1, HG_DK, HG_DK), lambda h, i: (0, i, h, 0, 0))],
        out_shape=[jax.ShapeDtypeStruct((b_s * t_s, d), F32),
                   jax.ShapeDtypeStruct((1, b_s, heads, HG_DK, HG_DK), F32)],
        compiler_params=_cparams(("arbitrary", "arbitrary"), 32),
        name="hgrn_rec_sample",
    )(proj_s, proj_s, proj_s, state_s)
    return jnp.concatenate([o_p, o_s], axis=0), s_p, s_s


def _hgrn_out_kernel(o_ref, sg_ref, ng_ref, w_ref, x_ref, g_ref, b_ref, rw_ref, rb_ref,
                     h_ref, idx_ref, gate_ref, *, alpha):
    sub = o_ref.shape[0] // OUT_SPLIT
    for c in range(OUT_SPLIT):
        rows = slice(c * sub, (c + 1) * sub)
        z = (o_ref[rows, :] * ng_ref[...] * sg_ref[rows, :]).astype(BF16)
        y = jnp.dot(z, w_ref[...], preferred_element_type=F32)
        _mix_post(x_ref[rows, :], y, alpha, g_ref, b_ref, rw_ref, rb_ref, h_ref, idx_ref, gate_ref, rows)


def _hgrn_out(o, proj, norm_g, w_out, x, ln_g, ln_b, router_w, router_b, alpha):
    n, d = x.shape
    n_e = router_w.shape[1]
    tm = 464
    assert n % tm == 0 and tm % (8 * OUT_SPLIT) == 0
    consts = (ln_g.reshape(1, d), ln_b.reshape(1, d), router_w, router_b.reshape(1, n_e))
    row = lambda i: (i, 0)
    return pl.pallas_call(
        functools.partial(_hgrn_out_kernel, alpha=alpha),
        grid=(n // tm,),
        in_specs=[pl.BlockSpec((tm, d), row),
                  pl.BlockSpec((tm, d), lambda i: (i, 3)),
                  _const_spec((1, d)),
                  _const_spec((d, d)),
                  pl.BlockSpec((tm, d), row)] + [_const_spec(c.shape) for c in consts],
        out_specs=[pl.BlockSpec((tm, d), row), pl.BlockSpec((tm, TOP_K), row),
                   pl.BlockSpec((tm, TOP_K), row)],
        out_shape=[jax.ShapeDtypeStruct((n, d), F32),
                   jax.ShapeDtypeStruct((n, TOP_K), jnp.int32),
                   jax.ShapeDtypeStruct((n, TOP_K), F32)],
        compiler_params=_cparams(("arbitrary",), 56),
        name="hgrn_out",
    )(o, proj, norm_g.reshape(1, d), w_out.astype(BF16), x, *consts)


def kernel(x_prompt, x_sample, state_pool, state_hgrn, meta_tokens, pool_w, pool_scale, hg_w_in, hg_lb, hg_norm_g, hg_w_out, ln_g, ln_b, router_w, router_b, w_gate, b_gate, w_up, b_up, w_down, b_down):
    depth = ln_g.shape[0]
    assert depth == 2, "layer 0 = pooling mixer, layer 1 = HGRN2 mixer"
    alpha = float((2 * depth) ** 0.25)
    b_p, seq, d = x_prompt.shape
    b_s, t_s, _ = x_sample.shape
    t_p = seq + N_META
    n_p = b_p * t_p

    xp = jnp.concatenate([jnp.broadcast_to(meta_tokens[None], (b_p, N_META, d)), x_prompt], axis=1)
    prefix = state_pool[0]
    xs_ext = jnp.concatenate([jnp.zeros((b_s, 16 - POOL_BUF, d), F32), prefix, x_sample], axis=1)

    h, idx, gates = _pool_layer(xp, xs_ext, pool_w[0], pool_scale[0], ln_g[0, 0], ln_b[0, 0],
                                router_w[0], router_b[0], alpha)
    moe_w = (w_gate, b_gate, w_up, b_up, w_down, b_down)
    h = _moe_layer(h, idx, gates, ln_g[0, 1], ln_b[0, 1], 0, *moe_w, alpha)

    sm = jax.nn.softmax(hg_lb.astype(F32), axis=0)
    lb = (jnp.cumsum(sm, axis=0) - sm[0:1])[1]
    proj = _hgrn_proj(h, hg_w_in[0], lb)
    o, s_p, s_s = _hgrn_recurrence(proj, state_hgrn, b_p, t_p, b_s, t_s)
    h, idx, gates = _hgrn_out(o, proj, hg_norm_g[0], hg_w_out[0], h, ln_g[1, 0], ln_b[1, 0],
                              router_w[1], router_b[1], alpha)
    h = _moe_layer(h, idx, gates, ln_g[1, 1], ln_b[1, 1], 1, *moe_w, alpha)

    y_prompt = h[:n_p].reshape(b_p, t_p, d)[:, N_META:]
    y_sample = h[n_p:].reshape(b_s, t_s, d)
    pool_p = x_prompt[:, seq - POOL_BUF:][None]
    pool_s = jnp.concatenate([prefix, x_sample], axis=1)[:, t_s:][None]
    return (y_prompt, y_sample, pool_p, pool_s, s_p[None], s_s)
```

```python
import functools

import jax
import jax.numpy as jnp
from jax import lax
from jax.experimental import pallas as pl
from jax.experimental.pallas import tpu as pltpu

N_META = 16
POOL_WINDOWS = (2, 4, 8, 16)
POOL_BUF = max(POOL_WINDOWS) - 1
PAST_LEN = 16384
HG_DK = 128
SUB = 16
TOP_K = 4
SWIGLU_LIMIT = 7.0
SWIGLU_ALPHA = 1.702
LN_EPS = 1e-5
RMS_EPS = 1e-6

PAD_TILE = 128
CHUNKS = (512, 256, 128)
SB_ROWS = 1536
FF_TILE = 512
N_FF_TILES = 4
REC_BLOCK = 128
REC_HEADS = 4
REC_SEQS = 8
DISPATCH_TILE = 640
COMBINE_TOKENS = 160
ISSUE_UNROLL = 4
PROJ_SPLIT = 2
OUT_SPLIT = 2
MIB = 1024 * 1024

F32 = jnp.float32
BF16 = jnp.bfloat16


def _cparams(sem, vmem_mib):
    return pltpu.CompilerParams(dimension_semantics=sem, vmem_limit_bytes=vmem_mib * MIB)


def _layer_norm(x, g, b):
    mu = jnp.mean(x, axis=-1, keepdims=True)
    xc = x - mu
    var = jnp.mean(xc * xc, axis=-1, keepdims=True)
    return xc * lax.rsqrt(var + LN_EPS) * g + b


def _dot_3pass(a, b):
    a_hi = a.astype(BF16)
    a_lo = (a - a_hi.astype(F32)).astype(BF16)
    b_hi = b.astype(BF16)
    b_lo = (b - b_hi.astype(F32)).astype(BF16)

    def d(x, y):
        return jnp.dot(x, y, preferred_element_type=F32)
    return d(a_hi, b_hi) + (d(a_hi, b_lo) + d(a_lo, b_hi))


def _route_rows(h, rw, rb):
    logits = _dot_3pass(h, rw) + rb
    n_e = logits.shape[-1]
    lane = lax.broadcasted_iota(jnp.int32, logits.shape, 1)
    vals, ids = [], []
    l = logits
    for _ in range(TOP_K):
        m = jnp.max(l, axis=-1, keepdims=True)
        ix = jnp.min(jnp.where(l == m, lane, n_e), axis=-1, keepdims=True)
        vals.append(m)
        ids.append(ix)
        l = jnp.where(lane == ix, -jnp.inf, l)
    v = jnp.concatenate(vals, axis=1)
    e = jnp.exp(v - vals[0])
    gates = e / jnp.sum(e, axis=-1, keepdims=True)
    return jnp.concatenate(ids, axis=1), gates


def _mix_post(x, y, alpha, g_ref, b_ref, rw_ref, rb_ref, h_ref, idx_ref, gate_ref, rows=slice(None)):
    h = _layer_norm(alpha * x + y, g_ref[...], b_ref[...])
    h_ref[rows, :] = h
    ids, gates = _route_rows(h, rw_ref[...], rb_ref[...])
    idx_ref[rows, :] = ids
    gate_ref[rows, :] = gates


def _window_sum(e, w):
    s = e
    sh = 1
    while sh < w:
        s = s + pltpu.roll(s, sh, axis=0)
        sh *= 2
    return s


def _pool_prompt_kernel(x_ref, halo_ref, pw_ref, ps_ref, g_ref, b_ref, rw_ref, rb_ref,
                        h_ref, idx_ref, gate_ref, *, tt, alpha):
    i = pl.program_id(1)
    cur = x_ref[0]
    halo = jnp.where(i > 0, halo_ref[0], 0.0)
    ext = jnp.concatenate([halo, cur], axis=0)
    grp = cur.shape[1] // len(POOL_WINDOWS)
    pos = i * tt + lax.broadcasted_iota(jnp.int32, (tt, 1), 0)
    ys = []
    for gi, w in enumerate(POOL_WINDOWS):
        s = _window_sum(ext[:, gi * grp:(gi + 1) * grp], w)[16:]
        cnt = jnp.minimum(w, pos + 1).astype(F32)
        d = s / cnt - cur[:, gi * grp:(gi + 1) * grp]
        ys.append(_dot_3pass(d, pw_ref[gi]))
    y = jnp.concatenate(ys, axis=1) * ps_ref[...]
    _mix_post(cur, y, alpha, g_ref, b_ref, rw_ref, rb_ref, h_ref, idx_ref, gate_ref)


def _pool_sample_kernel(x_ref, pw_ref, ps_ref, g_ref, b_ref, rw_ref, rb_ref, h_all, idx_all, gate_all,
                        h_ref, idx_ref, gate_ref, *, alpha, t_new):
    del h_all, idx_all, gate_all
    sb, te, d_model = x_ref.shape
    ext = x_ref[...].reshape(sb * te, d_model)
    grp = d_model // len(POOL_WINDOWS)
    cur = x_ref[:, 16:, :].reshape(sb * t_new, d_model)
    tpos = lax.broadcasted_iota(jnp.int32, (sb, t_new, 1), 1).reshape(sb * t_new, 1)
    ys = []
    for gi, w in enumerate(POOL_WINDOWS):
        s = _window_sum(ext[:, gi * grp:(gi + 1) * grp], w)
        s = s.reshape(sb, te, grp)[:, 16:, :].reshape(sb * t_new, grp)
        cnt = jnp.minimum(w, PAST_LEN + tpos + 1).astype(F32)
        d = s / cnt - cur[:, gi * grp:(gi + 1) * grp]
        ys.append(_dot_3pass(d, pw_ref[gi]))
    y = jnp.concatenate(ys, axis=1) * ps_ref[...]
    _mix_post(cur, y, alpha, g_ref, b_ref, rw_ref, rb_ref, h_ref, idx_ref, gate_ref)


def _const_spec(shape):
    nd = len(shape)
    return pl.BlockSpec(shape, lambda *_: (0,) * nd)


def _pool_layer(xp, xs_ext, pool_w, pool_scale, ln_g, ln_b, router_w, router_b, alpha):
    b_p, t_p, d = xp.shape
    n_e = router_w.shape[1]
    tt = 688
    assert t_p % tt == 0 and tt % 16 == 0
    ps = pool_scale.reshape(1, d)
    g2, b2, rb2 = ln_g.reshape(1, d), ln_b.reshape(1, d), router_b.reshape(1, n_e)
    consts = (pool_w, ps, g2, b2, router_w, rb2)
    const_specs = [_const_spec(c.shape) for c in consts]
    n_p = b_p * t_p
    nt = t_p // tt
    b_s, te, _ = xs_ext.shape
    t_new = te - 16
    sb = 8
    n_all = n_p + b_s * t_new
    out_shape = [jax.ShapeDtypeStruct((n_all, d), F32),
                 jax.ShapeDtypeStruct((n_all, TOP_K), jnp.int32),
                 jax.ShapeDtypeStruct((n_all, TOP_K), F32)]
    hp, ip, gp = pl.pallas_call(
        functools.partial(_pool_prompt_kernel, tt=tt, alpha=alpha),
        grid=(b_p, nt),
        in_specs=[pl.BlockSpec((1, tt, d), lambda b, i: (b, i, 0)),
                  pl.BlockSpec((1, 16, d), lambda b, i: (b, jnp.maximum(i * (tt // 16) - 1, 0), 0))]
        + const_specs,
        out_specs=[pl.BlockSpec((tt, d), lambda b, i: (b * nt + i, 0)),
                   pl.BlockSpec((tt, TOP_K), lambda b, i: (b * nt + i, 0)),
                   pl.BlockSpec((tt, TOP_K), lambda b, i: (b * nt + i, 0))],
        out_shape=out_shape,
        compiler_params=_cparams(("arbitrary", "arbitrary"), 56),
        name="pool_prompt",
    )(xp, xp, *consts)

    rows_s = sb * t_new
    assert b_s % sb == 0 and n_p % rows_s == 0
    blk0 = n_p // rows_s
    n_in = 1 + len(consts)
    any_spec = pl.BlockSpec(memory_space=pl.ANY)
    return pl.pallas_call(
        functools.partial(_pool_sample_kernel, alpha=alpha, t_new=t_new),
        grid=(b_s // sb,),
        in_specs=[pl.BlockSpec((sb, te, d), lambda i: (i, 0, 0))] + const_specs + [any_spec] * 3,
        out_specs=[pl.BlockSpec((rows_s, d), lambda i: (blk0 + i, 0)),
                   pl.BlockSpec((rows_s, TOP_K), lambda i: (blk0 + i, 0)),
                   pl.BlockSpec((rows_s, TOP_K), lambda i: (blk0 + i, 0))],
        out_shape=out_shape,
        input_output_aliases={n_in: 0, n_in + 1: 1, n_in + 2: 2},
        compiler_params=_cparams(("arbitrary",), 40),
        name="pool_sample",
    )(xs_ext, *consts, hp, ip, gp)


def _routing_tables(idx, n_experts):
    n, k = idx.shape
    a = n * k
    e_flat = idx.reshape(a)
    experts = jnp.arange(n_experts, dtype=jnp.int32)
    onehot = (e_flat[:, None] == experts[None, :]).astype(jnp.int32)
    csum = jnp.cumsum(onehot, axis=0)
    counts = csum[-1]
    pcounts = ((counts + PAD_TILE - 1) // PAD_TILE) * PAD_TILE
    pend = jnp.cumsum(pcounts)
    poff = pend - pcounts
    dest = jnp.sum(onehot * (csum - onehot + poff[None, :]), axis=1)

    m_rows = -(-(a + n_experts * (PAD_TILE - 1)) // PAD_TILE) * PAD_TILE
    n_pad = n_experts * (PAD_TILE - 1)
    n_pad_entries = -(-n_pad // DISPATCH_TILE) * DISPATCH_TILE
    j = jnp.arange(PAD_TILE - 1, dtype=jnp.int32)[None, :]
    trash = m_rows + jnp.arange(n_pad_entries, dtype=jnp.int32)
    pad_dest = jnp.where(j < (pcounts - counts)[:, None], (poff + counts)[:, None] + j,
                         trash[:n_pad].reshape(n_experts, PAD_TILE - 1)).reshape(n_pad)
    dest_all = jnp.concatenate([dest, pad_dest, trash[n_pad:]]).astype(jnp.int32)
    m_total = m_rows + n_pad_entries

    sb_rows = SB_ROWS
    s_max = n_experts + m_rows // sb_rows
    nsb = (pcounts + sb_rows - 1) // sb_rows
    sb_end = jnp.cumsum(nsb)
    n_used = sb_end[-1]
    s = jnp.arange(s_max, dtype=jnp.int32)
    s_eff = jnp.minimum(s, n_used - 1)
    e_of_s = jnp.sum((s_eff[:, None] >= sb_end[None, :]).astype(jnp.int32), axis=1)
    e_of_s = jnp.minimum(e_of_s, n_experts - 1)
    oh_s = (e_of_s[:, None] == experts[None, :]).astype(jnp.int32)
    local = s_eff - jnp.sum(oh_s * (sb_end - nsb)[None, :], axis=1)
    row0 = jnp.sum(oh_s * poff[None, :], axis=1) + local * sb_rows
    rows = jnp.clip(jnp.sum(oh_s * pcounts[None, :], axis=1) - local * sb_rows, 0, sb_rows)
    rows = jnp.where(s < n_used, rows, 0)
    return dict(dest=dest.astype(jnp.int32), dest_all=dest_all, m_total=m_total,
                sb_e=e_of_s.astype(jnp.int32), sb_row0=row0.astype(jnp.int32),
                sb_rows=rows.astype(jnp.int32), n_used=n_used.reshape(1).astype(jnp.int32),
                s_max=s_max, n_main=a)


def _dispatch_kernel(dest_ref, h_ref, xs_ref, z_scr, sem, *, n_main_steps):
    i = pl.program_id(0)
    base = i * DISPATCH_TILE

    def row_copy(src_ref, src_row, a):
        return pltpu.make_async_copy(src_ref.at[pl.ds(src_row, 1)],
                                     xs_ref.at[pl.ds(dest_ref[base + a], 1)], sem)

    @pl.when(i == 0)
    def _():
        z_scr[...] = jnp.zeros_like(z_scr)

    @pl.when(i < n_main_steps)
    def _():
        def issue(t, c):
            for k in range(TOP_K):
                row_copy(h_ref, t, t * TOP_K + k).start()
            return c
        lax.fori_loop(0, DISPATCH_TILE // TOP_K, issue, 0, unroll=ISSUE_UNROLL)

    @pl.when(i >= n_main_steps)
    def _():
        def issue(a, c):
            row_copy(z_scr, 0, a).start()
            return c
        lax.fori_loop(0, DISPATCH_TILE, issue, 0, unroll=4 * ISSUE_UNROLL)

    all_rows = xs_ref.at[pl.ds(0, DISPATCH_TILE)]
    pltpu.make_async_copy(all_rows, all_rows, sem).wait()


def _dispatch(h, tabs):
    n, d = h.shape
    n_main = tabs["n_main"]
    assert n_main % DISPATCH_TILE == 0
    n_steps = tabs["dest_all"].shape[0] // DISPATCH_TILE
    n_main_steps = n_main // DISPATCH_TILE
    tt = DISPATCH_TILE // TOP_K
    return pl.pallas_call(
        functools.partial(_dispatch_kernel, n_main_steps=n_main_steps),
        grid_spec=pltpu.PrefetchScalarGridSpec(
            num_scalar_prefetch=1, grid=(n_steps,),
            in_specs=[pl.BlockSpec((tt, d), lambda i, dr: (jnp.minimum(i, n_main_steps - 1), 0))],
            out_specs=pl.BlockSpec(memory_space=pl.ANY),
            scratch_shapes=[pltpu.VMEM((8, d), F32), pltpu.SemaphoreType.DMA(())]),
        out_shape=jax.ShapeDtypeStruct((tabs["m_total"], d), F32),
        compiler_params=_cparams(("arbitrary",), 16),
        name="moe_dispatch",
    )(tabs["dest_all"], h)


def _expert_kernel(sbe_ref, row0_ref, rows_ref, nused_ref, xs_ref, wg_ref, wu_ref, wd_ref,
                   bg_ref, bu_ref, bd_ref, ys_ref, x_scr, h_scr, y_scr, x_sem, y_sem):
    del sbe_ref
    s = pl.program_id(0)
    j = pl.program_id(1)
    n_sb = pl.num_programs(0)
    rows = rows_ref[s]
    row0 = pl.multiple_of(row0_ref[s], PAD_TILE)
    rows_prev = rows_ref[jnp.maximum(s - 1, 0)]
    s_next = jnp.minimum(s + 1, n_sb - 1)
    rows_next = jnp.where(s + 1 < n_sb, rows_ref[s_next], 0)
    row0_next = pl.multiple_of(row0_ref[s_next], PAD_TILE)
    is_last = s == nused_ref[0] - 1
    big = CHUNKS[0]

    def for_chunks(fn, n_rows=rows):
        n_big = n_rows // big

        def body(c, carry):
            fn(pl.multiple_of(c * big, big), big)
            return carry
        lax.fori_loop(0, n_big, body, 0)
        off = n_big * big
        rem = n_rows - off
        for tm in CHUNKS[1:]:
            has = (rem & tm) != 0

            @pl.when(has)
            def _(off=off, tm=tm):
                fn(pl.multiple_of(off, tm), tm)
            off = off + jnp.where(has, tm, 0)

    def x_piece(p, base=row0):
        r = pl.multiple_of(p * PAD_TILE, PAD_TILE)
        return pltpu.make_async_copy(xs_ref.at[pl.ds(pl.multiple_of(base + r, PAD_TILE), PAD_TILE)],
                                     x_scr.at[pl.ds(r, PAD_TILE)], x_sem.at[p])

    def start_x(base, first_piece, n_rows):
        def start(p, carry):
            x_piece(p, base).start()
            return carry
        lax.fori_loop(first_piece, n_rows // PAD_TILE, start, 0)

    def y_copy(n, slot, r0, tm):
        return pltpu.make_async_copy(
            y_scr.at[slot, pl.ds(r0, tm)],
            ys_ref.at[pl.ds(pl.multiple_of(row0 + r0, PAD_TILE), tm), pl.ds(n * FF_TILE, FF_TILE)],
            y_sem.at[slot])

    @pl.when(rows > 0)
    def _():
        @pl.when((j == 0) & (s == 0))
        def _():
            start_x(row0, 0, rows)

        @pl.when(j == N_FF_TILES)
        def _():
            start_x(row0_next, rows // PAD_TILE, rows_next)

        @pl.when(j < N_FF_TILES)
        def _():
            def gate_up(r0, tm):
                @pl.when(j == 0)
                def _():
                    for q in range(tm // PAD_TILE):
                        x_piece(r0 // PAD_TILE + q).wait()
                x = x_scr[pl.ds(r0, tm), :]
                hg = jnp.dot(x, wg_ref[0], preferred_element_type=F32) + bg_ref[0]
                hu = jnp.dot(x, wu_ref[0], preferred_element_type=F32) + bu_ref[0]
                hg = jnp.minimum(hg, SWIGLU_LIMIT)
                hu = jnp.clip(hu, -SWIGLU_LIMIT, SWIGLU_LIMIT)
                act = (hu + 1.0) * hg * jax.nn.sigmoid(SWIGLU_ALPHA * hg)
                h_scr[j, pl.ds(r0, tm), :] = act.astype(BF16)

                @pl.when(j == N_FF_TILES - 1)
                def _():
                    for q in range(tm // PAD_TILE):
                        piece = r0 // PAD_TILE + q

                        @pl.when(piece < rows_next // PAD_TILE)
                        def _(piece=piece):
                            x_piece(piece, row0_next).start()
            for_chunks(gate_up)

        @pl.when(j >= N_FF_TILES)
        def _():
            slot = (j - N_FF_TILES) % 2

            def down(r0, tm):
                acc = bd_ref[0] + jnp.zeros((tm, FF_TILE), F32)
                for f in range(N_FF_TILES):
                    w = wd_ref[0, f * FF_TILE:(f + 1) * FF_TILE, :].astype(BF16)
                    acc = acc + jnp.dot(h_scr[f, pl.ds(r0, tm), :], w, preferred_element_type=F32)
                y_scr[slot, pl.ds(r0, tm), :] = acc
                for n in range(N_FF_TILES):
                    @pl.when(j == N_FF_TILES + n)
                    def _(n=n):
                        y_copy(n, n % 2, r0, tm).start()
            for_chunks(down)

            @pl.when(j > N_FF_TILES)
            def _():
                for_chunks(lambda r0, tm: y_copy(0, 1 - slot, r0, tm).wait())

            @pl.when((j == N_FF_TILES) & (s > 0))
            def _():
                for_chunks(lambda r0, tm: y_copy(0, 1 - slot, r0, tm).wait(), rows_prev)

            @pl.when((j == 2 * N_FF_TILES - 1) & is_last)
            def _():
                for_chunks(lambda r0, tm: y_copy(0, slot, r0, tm).wait())


def _experts(xs, tabs, layer, w_gate, b_gate, w_up, b_up, w_down, b_down):
    n_l, n_e, d, f = w_gate.shape
    assert f == N_FF_TILES * FF_TILE and d == N_FF_TILES * FF_TILE
    w_gate, w_up, w_down = (w.reshape(n_l * n_e, *w.shape[2:]) for w in (w_gate, w_up, w_down))
    b_gate, b_up, b_down = (b.reshape(n_l * n_e, 1, b.shape[2]) for b in (b_gate, b_up, b_down))
    m_total = xs.shape[0]
    sb_rows = SB_ROWS
    last = N_FF_TILES - 1

    def up_map(s, j, sbe, r0, nrows, nu):
        return (sbe[s], 0, jnp.where(s < nu[0], jnp.minimum(j, last), last))

    def down_map(s, j, sbe, r0, nrows, nu):
        used = s < nu[0]
        in_down = j >= N_FF_TILES
        hold_prev = used & jnp.logical_not(in_down) & (s > 0)
        expert = jnp.where(hold_prev, sbe[jnp.maximum(s - 1, 0)], sbe[s])
        tile = jnp.where(in_down, j - N_FF_TILES, jnp.where(s > 0, last, 0))
        return (expert, 0, jnp.where(used, tile, last))

    return pl.pallas_call(
        _expert_kernel,
        grid_spec=pltpu.PrefetchScalarGridSpec(
            num_scalar_prefetch=4, grid=(tabs["s_max"], 2 * N_FF_TILES),
            in_specs=[pl.BlockSpec(memory_space=pl.ANY),
                      pl.BlockSpec((1, d, FF_TILE), up_map),
                      pl.BlockSpec((1, d, FF_TILE), up_map),
                      pl.BlockSpec((1, f, FF_TILE), down_map),
                      pl.BlockSpec((1, 1, FF_TILE), up_map),
                      pl.BlockSpec((1, 1, FF_TILE), up_map),
                      pl.BlockSpec((1, 1, FF_TILE), down_map)],
            out_specs=pl.BlockSpec(memory_space=pl.ANY),
            scratch_shapes=[pltpu.VMEM((sb_rows, d), F32),
                            pltpu.VMEM((N_FF_TILES, sb_rows, FF_TILE), BF16),
                            pltpu.VMEM((2, sb_rows, FF_TILE), F32),
                            pltpu.SemaphoreType.DMA((SB_ROWS // PAD_TILE,)),
                            pltpu.SemaphoreType.DMA((2,))]),
        out_shape=jax.ShapeDtypeStruct((m_total, d), F32),
        compiler_params=_cparams(("arbitrary", "arbitrary"), 56),
        name="moe_experts",
    )(tabs["sb_e"] + layer * n_e, tabs["sb_row0"], tabs["sb_rows"], tabs["n_used"], xs,
      w_gate, w_up, w_down, b_gate, b_up, b_down)


def _combine_kernel(dest_ref, ys_ref, h_ref, gate_ref, g_ref, b_ref, o_ref, buf, sem, *, alpha):
    i = pl.program_id(0)
    n_steps = pl.num_programs(0)
    tt = COMBINE_TOKENS

    def row_copy(step, slot, t, k):
        return pltpu.make_async_copy(
            ys_ref.at[pl.ds(dest_ref[(step * tt + t) * TOP_K + k], 1)],
            buf.at[slot, k, pl.ds(t, 1)], sem.at[slot])

    def issue(step, slot):
        def body(t, c):
            for k in range(TOP_K):
                row_copy(step, slot, t, k).start()
            return c
        lax.fori_loop(0, tt, body, 0, unroll=ISSUE_UNROLL)

    @pl.when(i == 0)
    def _():
        issue(0, 0)

    @pl.when(i + 1 < n_steps)
    def _():
        issue(i + 1, (i + 1) % 2)

    slot = i % 2

    pltpu.make_async_copy(buf.at[slot], buf.at[slot], sem.at[slot]).wait()

    gates = gate_ref[...]
    y = gates[:, 0:1] * buf[slot, 0]
    for k in range(1, TOP_K):
        y = y + gates[:, k:k + 1] * buf[slot, k]
    o_ref[...] = _layer_norm(alpha * h_ref[...] + y, g_ref[...], b_ref[...])


def _combine(ys, dest, h, gates, ln_g, ln_b, alpha):
    n, d = h.shape
    tt = COMBINE_TOKENS
    assert n % tt == 0
    return pl.pallas_call(
        functools.partial(_combine_kernel, alpha=alpha),
        grid_spec=pltpu.PrefetchScalarGridSpec(
            num_scalar_prefetch=1, grid=(n // tt,),
            in_specs=[pl.BlockSpec(memory_space=pl.ANY),
                      pl.BlockSpec((tt, d), lambda i, dr: (i, 0)),
                      pl.BlockSpec((tt, TOP_K), lambda i, dr: (i, 0)),
                      pl.BlockSpec((1, d), lambda i, dr: (0, 0)),
                      pl.BlockSpec((1, d), lambda i, dr: (0, 0))],
            out_specs=pl.BlockSpec((tt, d), lambda i, dr: (i, 0)),
            scratch_shapes=[pltpu.VMEM((2, TOP_K, tt, d), F32),
                            pltpu.SemaphoreType.DMA((2,))]),
        out_shape=jax.ShapeDtypeStruct((n, d), F32),
        compiler_params=_cparams(("arbitrary",), 40),
        name="moe_combine",
    )(dest, ys, h, gates, ln_g.reshape(1, d), ln_b.reshape(1, d))


def _moe_layer(h, idx, gates, ln_g, ln_b, layer, w_gate, b_gate, w_up, b_up, w_down, b_down, alpha):
    tabs = _routing_tables(idx, w_gate.shape[1])
    xs = _dispatch(h, tabs)
    ys = _experts(xs, tabs, layer, w_gate, b_gate, w_up, b_up, w_down, b_down)
    return _combine(ys, tabs["dest"], h, gates, ln_g, ln_b, alpha)


def _proj_kernel(x_ref, w_ref, lb_ref, o_ref, *, tiles_per_section):
    sec = pl.program_id(1) // tiles_per_section
    lb = lb_ref[...]
    sub = x_ref.shape[0] // PROJ_SPLIT
    for c in range(PROJ_SPLIT):
        rows = slice(c * sub, (c + 1) * sub)
        p = jnp.dot(x_ref[rows, :], w_ref[...], preferred_element_type=F32)
        sig = jax.nn.sigmoid(p)
        forget = lb + (1.0 - lb) * sig
        o_ref[rows, :] = jnp.where(sec == 1, forget, jnp.where(sec == 2, p, p * sig))


def _hgrn_proj(h, w_in, lb):
    n, d = h.shape
    tn = 1024
    tm = 928
    assert n % tm == 0 and d % tn == 0 and w_in.shape[1] == 4 * d and tm % (8 * PROJ_SPLIT) == 0
    tps = d // tn
    return pl.pallas_call(
        functools.partial(_proj_kernel, tiles_per_section=tps),
        grid=(n // tm, 4 * tps),
        in_specs=[pl.BlockSpec((tm, d), lambda i, j: (i, 0)),
                  pl.BlockSpec((d, tn), lambda i, j: (0, j)),
                  pl.BlockSpec((1, tn), lambda i, j: (0, j % tps))],
        out_specs=pl.BlockSpec((tm, tn), lambda i, j: (i, j)),
        out_shape=jax.ShapeDtypeStruct((n, 4 * d), F32),
        compiler_params=_cparams(("arbitrary", "arbitrary"), 48),
        name="hgrn_proj",
    )(h, w_in, lb.reshape(1, d))


_NT = (((1,), (1,)), ((), ()))
_TN = (((0,), (0,)), ((), ()))


def _hgrn_block(q, f, v, st):
    c = q.shape[0]
    n_sub = c // SUB
    lf = jnp.log(f)
    k = 1.0 - f
    row = lax.broadcasted_iota(jnp.int32, (c, 1), 0)
    loc = row % SUB
    bl = lf
    for sh in (1, 2, 4, 8):
        bl = bl + jnp.where(loc >= sh, pltpu.roll(bl, sh, axis=0), 0.0)
    tot = [bl[SUB * i + SUB - 1:SUB * i + SUB, :] for i in range(n_sub)]
    off = [jnp.zeros_like(tot[0])]
    for i in range(1, n_sub):
        off.append(off[-1] + tot[i - 1])
    b_last = off[-1] + tot[-1]
    if n_sub > 1:
        blk = row // SUB
        offs = off[0]
        tots = tot[0]
        for i in range(1, n_sub):
            offs = jnp.where(blk == i, off[i], offs)
            tots = jnp.where(blk == i, tot[i], tots)
        b = bl + offs
    else:
        b = bl
        tots = tot[0]
    r2 = lax.broadcasted_iota(jnp.int32, (c, c), 0)
    c2 = lax.broadcasted_iota(jnp.int32, (c, c), 1)
    a_loc = lax.dot_general(q * jnp.exp(bl), k * jnp.exp(-bl), _NT, preferred_element_type=F32)
    a = jnp.where((r2 // SUB == c2 // SUB) & (c2 <= r2), a_loc, 0.0)
    if n_sub > 1:
        k_end = k * jnp.exp(tots - bl)
        for jb in range(n_sub - 1):
            gamma = off[jb] + tot[jb]
            q_rel = q * jnp.exp(jnp.minimum(b - gamma, 0.0))
            a_j = lax.dot_general(q_rel, k_end, _NT, preferred_element_type=F32)
            a = jnp.where((c2 // SUB == jb) & (r2 // SUB > jb), a_j, a)
    o = jnp.dot(a, v, preferred_element_type=F32)
    o = o + lax.dot_general(q * jnp.exp(b), st, _NT, preferred_element_type=F32)
    kd = k * jnp.exp(b_last - b)
    st_new = st * jnp.exp(b_last) + lax.dot_general(v, kd, _TN, preferred_element_type=F32)
    o = o * lax.rsqrt(jnp.mean(o * o, axis=-1, keepdims=True) + RMS_EPS)
    return o, st_new


def _rec_prompt_kernel(q_ref, f_ref, v_ref, o_ref, s_ref, st_scr, *, n_blocks):
    st_scr[...] = jnp.zeros_like(st_scr)

    def run(r0, c):
        rows = pl.ds(r0, c)
        for hd in range(REC_HEADS):
            cols = slice(hd * HG_DK, (hd + 1) * HG_DK)
            o, st = _hgrn_block(q_ref[rows, cols], f_ref[rows, cols], v_ref[rows, cols], st_scr[hd])
            o_ref[rows, cols] = o
            st_scr[hd] = st

    run(0, SUB)

    def body(i, carry):
        run(pl.multiple_of(SUB + i * REC_BLOCK, SUB), REC_BLOCK)
        return carry
    lax.fori_loop(0, n_blocks, body, 0, unroll=2)
    for hd in range(REC_HEADS):
        s_ref[0, hd] = st_scr[hd].T


def _rec_sample_kernel(q_ref, f_ref, v_ref, s0_ref, o_ref, s_ref, *, t_new):
    sb = s0_ref.shape[1]

    def body(i, carry):
        for u in range(REC_SEQS):
            sq = i * REC_SEQS + u
            rows = pl.ds(pl.multiple_of(sq * t_new, t_new), t_new)
            o, st = _hgrn_block_short(q_ref[rows, :], f_ref[rows, :], v_ref[rows, :],
                                      s0_ref[0, sq, 0].T)
            o_ref[rows, :] = o
            s_ref[0, sq, 0] = st.T
        return carry
    lax.fori_loop(0, sb // REC_SEQS, body, 0)


def _hgrn_block_short(q, f, v, st):
    c = q.shape[0]
    lf = jnp.log(f)
    k = 1.0 - f
    row = lax.broadcasted_iota(jnp.int32, (c, 1), 0)
    bl = lf
    sh = 1
    while sh < c:
        bl = bl + jnp.where(row >= sh, pltpu.roll(bl, sh, axis=0), 0.0)
        sh *= 2
    b_last = bl[c - 1:c, :]
    r2 = lax.broadcasted_iota(jnp.int32, (c, c), 0)
    c2 = lax.broadcasted_iota(jnp.int32, (c, c), 1)
    qe = q * jnp.exp(bl)
    a = lax.dot_general(qe, k * jnp.exp(-bl), _NT, preferred_element_type=F32)
    a = jnp.where(c2 <= r2, a, 0.0)
    o = jnp.dot(a, v, preferred_element_type=F32)
    o = o + lax.dot_general(qe, st, _NT, preferred_element_type=F32)
    kd = k * jnp.exp(b_last - bl)
    st_new = st * jnp.exp(b_last) + lax.dot_general(v, kd, _TN, preferred_element_type=F32)
    o = o * lax.rsqrt(jnp.mean(o * o, axis=-1, keepdims=True) + RMS_EPS)
    return o, st_new


def _hgrn_recurrence(proj, state_s, b_p, t_p, b_s, t_s):
    n, d4 = proj.shape
    d = d4 // 4
    heads = d // HG_DK
    n_p = b_p * t_p
    assert (t_p - SUB) % REC_BLOCK == 0 and t_s < SUB and t_s % 8 == 0
    assert heads % REC_HEADS == 0
    hgroups = heads // REC_HEADS
    gw = REC_HEADS * HG_DK
    o_p, s_p = pl.pallas_call(
        functools.partial(_rec_prompt_kernel, n_blocks=(t_p - SUB) // REC_BLOCK),
        grid=(b_p, hgroups),
        in_specs=[pl.BlockSpec((t_p, gw), lambda b, h: (b, h)),
                  pl.BlockSpec((t_p, gw), lambda b, h: (b, hgroups + h)),
                  pl.BlockSpec((t_p, gw), lambda b, h: (b, 2 * hgroups + h))],
        out_specs=[pl.BlockSpec((t_p, gw), lambda b, h: (b, h)),
                   pl.BlockSpec((1, REC_HEADS, HG_DK, HG_DK), lambda b, h: (b, h, 0, 0))],
        out_shape=[jax.ShapeDtypeStruct((n_p, d), F32),
                   jax.ShapeDtypeStruct((b_p, heads, HG_DK, HG_DK), F32)],
        scratch_shapes=[pltpu.VMEM((REC_HEADS, HG_DK, HG_DK), F32)],
        compiler_params=_cparams(("arbitrary", "arbitrary"), 48),
        name="hgrn_rec_prompt",
    )(proj, proj, proj)

    proj_s = proj[n_p:]
    sb = 32
    assert b_s % sb == 0
    o_s, s_s = pl.pallas_call(
        functools.partial(_rec_sample_kernel, t_new=t_s),
        grid=(heads, b_s // sb),
        in_specs=[pl.BlockSpec((sb * t_s, HG_DK), lambda h, i: (i, h)),
                  pl.BlockSpec((sb * t_s, HG_DK), lambda h, i: (i, heads + h)),
                  pl.BlockSpec((sb * t_s, HG_DK), lambda h, i: (i, 2 * heads + h)),
                  pl.BlockSpec((1, sb, 1, HG_DK, HG_DK), lambda h, i: (0, i, h, 0, 0))],
        out_specs=[pl.BlockSpec((sb * t_s, HG_DK), lambda h, i: (i, h)),
                   pl.BlockSpec((1, sb, 1, HG_DK, HG_DK), lambda h, i: (0, i, h, 0, 0))],
        out_shape=[jax.ShapeDtypeStruct((b_s * t_s, d), F32),
                   jax.ShapeDtypeStruct((1, b_s, heads, HG_DK, HG_DK), F32)],
        compiler_params=_cparams(("arbitrary", "arbitrary"), 32),
        name="hgrn_rec_sample",
    )(proj_s, proj_s, proj_s, state_s)
    return jnp.concatenate([o_p, o_s], axis=0), s_p, s_s


def _hgrn_out_kernel(o_ref, sg_ref, ng_ref, w_ref, x_ref, g_ref, b_ref, rw_ref, rb_ref,
                     h_ref, idx_ref, gate_ref, *, alpha):
    sub = o_ref.shape[0] // OUT_SPLIT
    for c in range(OUT_SPLIT):
        rows = slice(c * sub, (c + 1) * sub)
        z = (o_ref[rows, :] * ng_ref[...] * sg_ref[rows, :]).astype(BF16)
        y = jnp.dot(z, w_ref[...], preferred_element_type=F32)
        _mix_post(x_ref[rows, :], y, alpha, g_ref, b_ref, rw_ref, rb_ref, h_ref, idx_ref, gate_ref, rows)


def _hgrn_out(o, proj, norm_g, w_out, x, ln_g, ln_b, router_w, router_b, alpha):
    n, d = x.shape
    n_e = router_w.shape[1]
    tm = 464
    assert n % tm == 0 and tm % (8 * OUT_SPLIT) == 0
    consts = (ln_g.reshape(1, d), ln_b.reshape(1, d), router_w, router_b.reshape(1, n_e))
    row = lambda i: (i, 0)
    return pl.pallas_call(
        functools.partial(_hgrn_out_kernel, alpha=alpha),
        grid=(n // tm,),
        in_specs=[pl.BlockSpec((tm, d), row),
                  pl.BlockSpec((tm, d), lambda i: (i, 3)),
                  _const_spec((1, d)),
                  _const_spec((d, d)),
                  pl.BlockSpec((tm, d), row)] + [_const_spec(c.shape) for c in consts],
        out_specs=[pl.BlockSpec((tm, d), row), pl.BlockSpec((tm, TOP_K), row),
                   pl.BlockSpec((tm, TOP_K), row)],
        out_shape=[jax.ShapeDtypeStruct((n, d), F32),
                   jax.ShapeDtypeStruct((n, TOP_K), jnp.int32),
                   jax.ShapeDtypeStruct((n, TOP_K), F32)],
        compiler_params=_cparams(("arbitrary",), 56),
        name="hgrn_out",
    )(o, proj, norm_g.reshape(1, d), w_out.astype(BF16), x, *consts)


def kernel(x_prompt, x_sample, state_pool, state_hgrn, meta_tokens, pool_w, pool_scale, hg_w_in, hg_lb, hg_norm_g, hg_w_out, ln_g, ln_b, router_w, router_b, w_gate, b_gate, w_up, b_up, w_down, b_down):
    depth = ln_g.shape[0]
    assert depth == 2, "layer 0 = pooling mixer, layer 1 = HGRN2 mixer"
    alpha = float((2 * depth) ** 0.25)
    b_p, seq, d = x_prompt.shape
    b_s, t_s, _ = x_sample.shape
    t_p = seq + N_META
    n_p = b_p * t_p

    xp = jnp.concatenate([jnp.broadcast_to(meta_tokens[None], (b_p, N_META, d)), x_prompt], axis=1)
    prefix = state_pool[0]
    xs_ext = jnp.concatenate([jnp.zeros((b_s, 16 - POOL_BUF, d), F32), prefix, x_sample], axis=1)

    h, idx, gates = _pool_layer(xp, xs_ext, pool_w[0], pool_scale[0], ln_g[0, 0], ln_b[0, 0],
                                router_w[0], router_b[0], alpha)
    moe_w = (w_gate, b_gate, w_up, b_up, w_down, b_down)
    h = _moe_layer(h, idx, gates, ln_g[0, 1], ln_b[0, 1], 0, *moe_w, alpha)

    sm = jax.nn.softmax(hg_lb.astype(F32), axis=0)
    lb = (jnp.cumsum(sm, axis=0) - sm[0:1])[1]
    proj = _hgrn_proj(h, hg_w_in[0], lb)
    o, s_p, s_s = _hgrn_recurrence(proj, state_hgrn, b_p, t_p, b_s, t_s)
    h, idx, gates = _hgrn_out(o, proj, hg_norm_g[0], hg_w_out[0], h, ln_g[1, 0], ln_b[1, 0],
                              router_w[1], router_b[1], alpha)
    h = _moe_layer(h, idx, gates, ln_g[1, 1], ln_b[1, 1], 1, *moe_w, alpha)

    y_prompt = h[:n_p].reshape(b_p, t_p, d)[:, N_META:]
    y_sample = h[n_p:].reshape(b_s, t_s, d)
    pool_p = x_prompt[:, seq - POOL_BUF:][None]
    pool_s = jnp.concatenate([prefix, x_sample], axis=1)[:, t_s:][None]
    return (y_prompt, y_sample, pool_p, pool_s, s_p[None], s_s)
```

```python
import functools

import jax
import jax.numpy as jnp
from jax import lax
from jax.experimental import pallas as pl
from jax.experimental.pallas import tpu as pltpu

N_META = 16
POOL_WINDOWS = (2, 4, 8, 16)
POOL_BUF = max(POOL_WINDOWS) - 1
PAST_LEN = 16384
HG_DK = 128
SUB = 16
TOP_K = 4
SWIGLU_LIMIT = 7.0
SWIGLU_ALPHA = 1.702
LN_EPS = 1e-5
RMS_EPS = 1e-6

PAD_TILE = 128
CHUNKS = (512, 256, 128)
SB_ROWS = 1536
FF_TILE = 512
N_FF_TILES = 4
REC_BLOCK = 128
REC_HEADS = 4
REC_SEQS = 16
DISPATCH_TILE = 1280
COMBINE_TOKENS = 320
ISSUE_UNROLL = 4
PROJ_SPLIT = 2
OUT_SPLIT = 2
MIB = 1024 * 1024

F32 = jnp.float32
BF16 = jnp.bfloat16


def _cparams(sem, vmem_mib):
    return pltpu.CompilerParams(dimension_semantics=sem, vmem_limit_bytes=vmem_mib * MIB)


def _layer_norm(x, g, b):
    mu = jnp.mean(x, axis=-1, keepdims=True)
    xc = x - mu
    var = jnp.mean(xc * xc, axis=-1, keepdims=True)
    return xc * lax.rsqrt(var + LN_EPS) * g + b


def _dot_3pass(a, b):
    a_hi = a.astype(BF16)
    a_lo = (a - a_hi.astype(F32)).astype(BF16)
    b_hi = b.astype(BF16)
    b_lo = (b - b_hi.astype(F32)).astype(BF16)

    def d(x, y):
        return jnp.dot(x, y, preferred_element_type=F32)
    return d(a_hi, b_hi) + (d(a_hi, b_lo) + d(a_lo, b_hi))


def _route_rows(h, rw, rb):
    logits = _dot_3pass(h, rw) + rb
    n_e = logits.shape[-1]
    lane = lax.broadcasted_iota(jnp.int32, logits.shape, 1)
    vals, ids = [], []
    l = logits
    for _ in range(TOP_K):
        m = jnp.max(l, axis=-1, keepdims=True)
        ix = jnp.min(jnp.where(l == m, lane, n_e), axis=-1, keepdims=True)
        vals.append(m)
        ids.append(ix)
        l = jnp.where(lane == ix, -jnp.inf, l)
    v = jnp.concatenate(vals, axis=1)
    e = jnp.exp(v - vals[0])
    gates = e / jnp.sum(e, axis=-1, keepdims=True)
    return jnp.concatenate(ids, axis=1), gates


def _mix_post(x, y, alpha, g_ref, b_ref, rw_ref, rb_ref, h_ref, idx_ref, gate_ref, rows=slice(None)):
    h = _layer_norm(alpha * x + y, g_ref[...], b_ref[...])
    h_ref[rows, :] = h
    ids, gates = _route_rows(h, rw_ref[...], rb_ref[...])
    idx_ref[rows, :] = ids
    gate_ref[rows, :] = gates


def _window_sum(e, w):
    s = e
    sh = 1
    while sh < w:
        s = s + pltpu.roll(s, sh, axis=0)
        sh *= 2
    return s


def _pool_prompt_kernel(x_ref, halo_ref, pw_ref, ps_ref, g_ref, b_ref, rw_ref, rb_ref,
                        h_ref, idx_ref, gate_ref, *, tt, alpha):
    i = pl.program_id(1)
    cur = x_ref[0]
    halo = jnp.where(i > 0, halo_ref[0], 0.0)
    ext = jnp.concatenate([halo, cur], axis=0)
    grp = cur.shape[1] // len(POOL_WINDOWS)
    pos = i * tt + lax.broadcasted_iota(jnp.int32, (tt, 1), 0)
    ys = []
    for gi, w in enumerate(POOL_WINDOWS):
        s = _window_sum(ext[:, gi * grp:(gi + 1) * grp], w)[16:]
        cnt = jnp.minimum(w, pos + 1).astype(F32)
        d = s / cnt - cur[:, gi * grp:(gi + 1) * grp]
        ys.append(_dot_3pass(d, pw_ref[gi]))
    y = jnp.concatenate(ys, axis=1) * ps_ref[...]
    _mix_post(cur, y, alpha, g_ref, b_ref, rw_ref, rb_ref, h_ref, idx_ref, gate_ref)


def _pool_sample_kernel(x_ref, pw_ref, ps_ref, g_ref, b_ref, rw_ref, rb_ref, h_all, idx_all, gate_all,
                        h_ref, idx_ref, gate_ref, *, alpha, t_new):
    del h_all, idx_all, gate_all
    sb, te, d_model = x_ref.shape
    ext = x_ref[...].reshape(sb * te, d_model)
    grp = d_model // len(POOL_WINDOWS)
    cur = x_ref[:, 16:, :].reshape(sb * t_new, d_model)
    tpos = lax.broadcasted_iota(jnp.int32, (sb, t_new, 1), 1).reshape(sb * t_new, 1)
    ys = []
    for gi, w in enumerate(POOL_WINDOWS):
        s = _window_sum(ext[:, gi * grp:(gi + 1) * grp], w)
        s = s.reshape(sb, te, grp)[:, 16:, :].reshape(sb * t_new, grp)
        cnt = jnp.minimum(w, PAST_LEN + tpos + 1).astype(F32)
        d = s / cnt - cur[:, gi * grp:(gi + 1) * grp]
        ys.append(_dot_3pass(d, pw_ref[gi]))
    y = jnp.concatenate(ys, axis=1) * ps_ref[...]
    _mix_post(cur, y, alpha, g_ref, b_ref, rw_ref, rb_ref, h_ref, idx_ref, gate_ref)


def _const_spec(shape):
    nd = len(shape)
    return pl.BlockSpec(shape, lambda *_: (0,) * nd)


def _pool_layer(xp, xs_ext, pool_w, pool_scale, ln_g, ln_b, router_w, router_b, alpha):
    b_p, t_p, d = xp.shape
    n_e = router_w.shape[1]
    tt = 688
    assert t_p % tt == 0 and tt % 16 == 0
    ps = pool_scale.reshape(1, d)
    g2, b2, rb2 = ln_g.reshape(1, d), ln_b.reshape(1, d), router_b.reshape(1, n_e)
    consts = (pool_w, ps, g2, b2, router_w, rb2)
    const_specs = [_const_spec(c.shape) for c in consts]
    n_p = b_p * t_p
    nt = t_p // tt
    b_s, te, _ = xs_ext.shape
    t_new = te - 16
    sb = 8
    n_all = n_p + b_s * t_new
    out_shape = [jax.ShapeDtypeStruct((n_all, d), F32),
                 jax.ShapeDtypeStruct((n_all, TOP_K), jnp.int32),
                 jax.ShapeDtypeStruct((n_all, TOP_K), F32)]
    hp, ip, gp = pl.pallas_call(
        functools.partial(_pool_prompt_kernel, tt=tt, alpha=alpha),
        grid=(b_p, nt),
        in_specs=[pl.BlockSpec((1, tt, d), lambda b, i: (b, i, 0)),
                  pl.BlockSpec((1, 16, d), lambda b, i: (b, jnp.maximum(i * (tt // 16) - 1, 0), 0))]
        + const_specs,
        out_specs=[pl.BlockSpec((tt, d), lambda b, i: (b * nt + i, 0)),
                   pl.BlockSpec((tt, TOP_K), lambda b, i: (b * nt + i, 0)),
                   pl.BlockSpec((tt, TOP_K), lambda b, i: (b * nt + i, 0))],
        out_shape=out_shape,
        compiler_params=_cparams(("arbitrary", "arbitrary"), 56),
        name="pool_prompt",
    )(xp, xp, *consts)

    rows_s = sb * t_new
    assert b_s % sb == 0 and n_p % rows_s == 0
    blk0 = n_p // rows_s
    n_in = 1 + len(consts)
    any_spec = pl.BlockSpec(memory_space=pl.ANY)
    return pl.pallas_call(
        functools.partial(_pool_sample_kernel, alpha=alpha, t_new=t_new),
        grid=(b_s // sb,),
        in_specs=[pl.BlockSpec((sb, te, d), lambda i: (i, 0, 0))] + const_specs + [any_spec] * 3,
        out_specs=[pl.BlockSpec((rows_s, d), lambda i: (blk0 + i, 0)),
                   pl.BlockSpec((rows_s, TOP_K), lambda i: (blk0 + i, 0)),
                   pl.BlockSpec((rows_s, TOP_K), lambda i: (blk0 + i, 0))],
        out_shape=out_shape,
        input_output_aliases={n_in: 0, n_in + 1: 1, n_in + 2: 2},
        compiler_params=_cparams(("arbitrary",), 40),
        name="pool_sample",
    )(xs_ext, *consts, hp, ip, gp)


def _routing_tables(idx, n_experts):
    n, k = idx.shape
    a = n * k
    e_flat = idx.reshape(a)
    experts = jnp.arange(n_experts, dtype=jnp.int32)
    onehot = (e_flat[:, None] == experts[None, :]).astype(jnp.int32)
    csum = jnp.cumsum(onehot, axis=0)
    counts = csum[-1]
    pcounts = ((counts + PAD_TILE - 1) // PAD_TILE) * PAD_TILE
    pend = jnp.cumsum(pcounts)
    poff = pend - pcounts
    dest = jnp.sum(onehot * (csum - onehot + poff[None, :]), axis=1)

    m_rows = -(-(a + n_experts * (PAD_TILE - 1)) // PAD_TILE) * PAD_TILE
    n_pad = n_experts * (PAD_TILE - 1)
    n_pad_entries = -(-n_pad // DISPATCH_TILE) * DISPATCH_TILE
    j = jnp.arange(PAD_TILE - 1, dtype=jnp.int32)[None, :]
    trash = m_rows + jnp.arange(n_pad_entries, dtype=jnp.int32)
    pad_dest = jnp.where(j < (pcounts - counts)[:, None], (poff + counts)[:, None] + j,
                         trash[:n_pad].reshape(n_experts, PAD_TILE - 1)).reshape(n_pad)
    dest_all = jnp.concatenate([dest, pad_dest, trash[n_pad:]]).astype(jnp.int32)
    m_total = m_rows + n_pad_entries

    sb_rows = SB_ROWS
    s_max = n_experts + m_rows // sb_rows
    nsb = (pcounts + sb_rows - 1) // sb_rows
    sb_end = jnp.cumsum(nsb)
    n_used = sb_end[-1]
    s = jnp.arange(s_max, dtype=jnp.int32)
    s_eff = jnp.minimum(s, n_used - 1)
    e_of_s = jnp.sum((s_eff[:, None] >= sb_end[None, :]).astype(jnp.int32), axis=1)
    e_of_s = jnp.minimum(e_of_s, n_experts - 1)
    oh_s = (e_of_s[:, None] == experts[None, :]).astype(jnp.int32)
    local = s_eff - jnp.sum(oh_s * (sb_end - nsb)[None, :], axis=1)
    row0 = jnp.sum(oh_s * poff[None, :], axis=1) + local * sb_rows
    rows = jnp.clip(jnp.sum(oh_s * pcounts[None, :], axis=1) - local * sb_rows, 0, sb_rows)
    rows = jnp.where(s < n_used, rows, 0)
    return dict(dest=dest.astype(jnp.int32), dest_all=dest_all, m_total=m_total,
                sb_e=e_of_s.astype(jnp.int32), sb_row0=row0.astype(jnp.int32),
                sb_rows=rows.astype(jnp.int32), n_used=n_used.reshape(1).astype(jnp.int32),
                s_max=s_max, n_main=a)


def _dispatch_kernel(dest_ref, h_ref, xs_ref, z_scr, sem, *, n_main_steps):
    i = pl.program_id(0)
    base = i * DISPATCH_TILE

    def row_copy(src_ref, src_row, a):
        return pltpu.make_async_copy(src_ref.at[pl.ds(src_row, 1)],
                                     xs_ref.at[pl.ds(dest_ref[base + a], 1)], sem)

    @pl.when(i == 0)
    def _():
        z_scr[...] = jnp.zeros_like(z_scr)

    @pl.when(i < n_main_steps)
    def _():
        def issue(t, c):
            for k in range(TOP_K):
                row_copy(h_ref, t, t * TOP_K + k).start()
            return c
        lax.fori_loop(0, DISPATCH_TILE // TOP_K, issue, 0, unroll=ISSUE_UNROLL)

    @pl.when(i >= n_main_steps)
    def _():
        def issue(a, c):
            row_copy(z_scr, 0, a).start()
            return c
        lax.fori_loop(0, DISPATCH_TILE, issue, 0, unroll=4 * ISSUE_UNROLL)

    all_rows = xs_ref.at[pl.ds(0, DISPATCH_TILE)]
    pltpu.make_async_copy(all_rows, all_rows, sem).wait()


def _dispatch(h, tabs):
    n, d = h.shape
    n_main = tabs["n_main"]
    assert n_main % DISPATCH_TILE == 0
    n_steps = tabs["dest_all"].shape[0] // DISPATCH_TILE
    n_main_steps = n_main // DISPATCH_TILE
    tt = DISPATCH_TILE // TOP_K
    return pl.pallas_call(
        functools.partial(_dispatch_kernel, n_main_steps=n_main_steps),
        grid_spec=pltpu.PrefetchScalarGridSpec(
            num_scalar_prefetch=1, grid=(n_steps,),
            in_specs=[pl.BlockSpec((tt, d), lambda i, dr: (jnp.minimum(i, n_main_steps - 1), 0))],
            out_specs=pl.BlockSpec(memory_space=pl.ANY),
            scratch_shapes=[pltpu.VMEM((8, d), F32), pltpu.SemaphoreType.DMA(())]),
        out_shape=jax.ShapeDtypeStruct((tabs["m_total"], d), F32),
        compiler_params=_cparams(("arbitrary",), 16),
        name="moe_dispatch",
    )(tabs["dest_all"], h)


def _expert_kernel(sbe_ref, row0_ref, rows_ref, nused_ref, xs_ref, wg_ref, wu_ref, wd_ref,
                   bg_ref, bu_ref, bd_ref, ys_ref, x_scr, h_scr, y_scr, x_sem, y_sem):
    del sbe_ref
    s = pl.program_id(0)
    j = pl.program_id(1)
    n_sb = pl.num_programs(0)
    rows = rows_ref[s]
    row0 = pl.multiple_of(row0_ref[s], PAD_TILE)
    rows_prev = rows_ref[jnp.maximum(s - 1, 0)]
    s_next = jnp.minimum(s + 1, n_sb - 1)
    rows_next = jnp.where(s + 1 < n_sb, rows_ref[s_next], 0)
    row0_next = pl.multiple_of(row0_ref[s_next], PAD_TILE)
    is_last = s == nused_ref[0] - 1
    big = CHUNKS[0]

    def for_chunks(fn, n_rows=rows):
        n_big = n_rows // big

        def body(c, carry):
            fn(pl.multiple_of(c * big, big), big)
            return carry
        lax.fori_loop(0, n_big, body, 0)
        off = n_big * big
        rem = n_rows - off
        for tm in CHUNKS[1:]:
            has = (rem & tm) != 0

            @pl.when(has)
            def _(off=off, tm=tm):
                fn(pl.multiple_of(off, tm), tm)
            off = off + jnp.where(has, tm, 0)

    def x_piece(p, base=row0):
        r = pl.multiple_of(p * PAD_TILE, PAD_TILE)
        return pltpu.make_async_copy(xs_ref.at[pl.ds(pl.multiple_of(base + r, PAD_TILE), PAD_TILE)],
                                     x_scr.at[pl.ds(r, PAD_TILE)], x_sem.at[p])

    def start_x(base, first_piece, n_rows):
        def start(p, carry):
            x_piece(p, base).start()
            return carry
        lax.fori_loop(first_piece, n_rows // PAD_TILE, start, 0)

    def y_copy(n, slot, r0, tm):
        return pltpu.make_async_copy(
            y_scr.at[slot, pl.ds(r0, tm)],
            ys_ref.at[pl.ds(pl.multiple_of(row0 + r0, PAD_TILE), tm), pl.ds(n * FF_TILE, FF_TILE)],
            y_sem.at[slot])

    @pl.when(rows > 0)
    def _():
        @pl.when((j == 0) & (s == 0))
        def _():
            start_x(row0, 0, rows)

        @pl.when(j == N_FF_TILES)
        def _():
            start_x(row0_next, rows // PAD_TILE, rows_next)

        @pl.when(j < N_FF_TILES)
        def _():
            def gate_up(r0, tm):
                @pl.when(j == 0)
                def _():
                    for q in range(tm // PAD_TILE):
                        x_piece(r0 // PAD_TILE + q).wait()
                x = x_scr[pl.ds(r0, tm), :]
                hg = jnp.dot(x, wg_ref[0], preferred_element_type=F32) + bg_ref[0]
                hu = jnp.dot(x, wu_ref[0], preferred_element_type=F32) + bu_ref[0]
                hg = jnp.minimum(hg, SWIGLU_LIMIT)
                hu = jnp.clip(hu, -SWIGLU_LIMIT, SWIGLU_LIMIT)
                act = (hu + 1.0) * hg * jax.nn.sigmoid(SWIGLU_ALPHA * hg)
                h_scr[j, pl.ds(r0, tm), :] = act.astype(BF16)

                @pl.when(j == N_FF_TILES - 1)
                def _():
                    for q in range(tm // PAD_TILE):
                        piece = r0 // PAD_TILE + q

                        @pl.when(piece < rows_next // PAD_TILE)
                        def _(piece=piece):
                            x_piece(piece, row0_next).start()
            for_chunks(gate_up)

        @pl.when(j >= N_FF_TILES)
        def _():
            slot = (j - N_FF_TILES) % 2

            def down(r0, tm):
                acc = bd_ref[0] + jnp.zeros((tm, FF_TILE), F32)
                for f in range(N_FF_TILES):
                    w = wd_ref[0, f * FF_TILE:(f + 1) * FF_TILE, :].astype(BF16)
                    acc = acc + jnp.dot(h_scr[f, pl.ds(r0, tm), :], w, preferred_element_type=F32)
                y_scr[slot, pl.ds(r0, tm), :] = acc
                for n in range(N_FF_TILES):
                    @pl.when(j == N_FF_TILES + n)
                    def _(n=n):
                        y_copy(n, n % 2, r0, tm).start()
            for_chunks(down)

            @pl.when(j > N_FF_TILES)
            def _():
                for_chunks(lambda r0, tm: y_copy(0, 1 - slot, r0, tm).wait())

            @pl.when((j == N_FF_TILES) & (s > 0))
            def _():
                for_chunks(lambda r0, tm: y_copy(0, 1 - slot, r0, tm).wait(), rows_prev)

            @pl.when((j == 2 * N_FF_TILES - 1) & is_last)
            def _():
                for_chunks(lambda r0, tm: y_copy(0, slot, r0, tm).wait())


def _experts(xs, tabs, layer, w_gate, b_gate, w_up, b_up, w_down, b_down):
    n_l, n_e, d, f = w_gate.shape
    assert f == N_FF_TILES * FF_TILE and d == N_FF_TILES * FF_TILE
    w_gate, w_up, w_down = (w.reshape(n_l * n_e, *w.shape[2:]) for w in (w_gate, w_up, w_down))
    b_gate, b_up, b_down = (b.reshape(n_l * n_e, 1, b.shape[2]) for b in (b_gate, b_up, b_down))
    m_total = xs.shape[0]
    sb_rows = SB_ROWS
    last = N_FF_TILES - 1

    def up_map(s, j, sbe, r0, nrows, nu):
        return (sbe[s], 0, jnp.where(s < nu[0], jnp.minimum(j, last), last))

    def down_map(s, j, sbe, r0, nrows, nu):
        used = s < nu[0]
        in_down = j >= N_FF_TILES
        hold_prev = used & jnp.logical_not(in_down) & (s > 0)
        expert = jnp.where(hold_prev, sbe[jnp.maximum(s - 1, 0)], sbe[s])
        tile = jnp.where(in_down, j - N_FF_TILES, jnp.where(s > 0, last, 0))
        return (expert, 0, jnp.where(used, tile, last))

    return pl.pallas_call(
        _expert_kernel,
        grid_spec=pltpu.PrefetchScalarGridSpec(
            num_scalar_prefetch=4, grid=(tabs["s_max"], 2 * N_FF_TILES),
            in_specs=[pl.BlockSpec(memory_space=pl.ANY),
                      pl.BlockSpec((1, d, FF_TILE), up_map),
                      pl.BlockSpec((1, d, FF_TILE), up_map),
                      pl.BlockSpec((1, f, FF_TILE), down_map),
                      pl.BlockSpec((1, 1, FF_TILE), up_map),
                      pl.BlockSpec((1, 1, FF_TILE), up_map),
                      pl.BlockSpec((1, 1, FF_TILE), down_map)],
            out_specs=pl.BlockSpec(memory_space=pl.ANY),
            scratch_shapes=[pltpu.VMEM((sb_rows, d), F32),
                            pltpu.VMEM((N_FF_TILES, sb_rows, FF_TILE), BF16),
                            pltpu.VMEM((2, sb_rows, FF_TILE), F32),
                            pltpu.SemaphoreType.DMA((SB_ROWS // PAD_TILE,)),
                            pltpu.SemaphoreType.DMA((2,))]),
        out_shape=jax.ShapeDtypeStruct((m_total, d), F32),
        compiler_params=_cparams(("arbitrary", "arbitrary"), 56),
        name="moe_experts",
    )(tabs["sb_e"] + layer * n_e, tabs["sb_row0"], tabs["sb_rows"], tabs["n_used"], xs,
      w_gate, w_up, w_down, b_gate, b_up, b_down)


def _combine_kernel(dest_ref, ys_ref, h_ref, gate_ref, g_ref, b_ref, o_ref, buf, sem, *, alpha):
    i = pl.program_id(0)
    n_steps = pl.num_programs(0)
    tt = COMBINE_TOKENS

    def row_copy(step, slot, t, k):
        return pltpu.make_async_copy(
            ys_ref.at[pl.ds(dest_ref[(step * tt + t) * TOP_K + k], 1)],
            buf.at[slot, k, pl.ds(t, 1)], sem.at[slot])

    def issue(step, slot):
        def body(t, c):
            for k in range(TOP_K):
                row_copy(step, slot, t, k).start()
            return c
        lax.fori_loop(0, tt, body, 0, unroll=ISSUE_UNROLL)

    @pl.when(i == 0)
    def _():
        issue(0, 0)

    @pl.when(i + 1 < n_steps)
    def _():
        issue(i + 1, (i + 1) % 2)

    slot = i % 2

    pltpu.make_async_copy(buf.at[slot], buf.at[slot], sem.at[slot]).wait()

    gates = gate_ref[...]
    y = gates[:, 0:1] * buf[slot, 0]
    for k in range(1, TOP_K):
        y = y + gates[:, k:k + 1] * buf[slot, k]
    o_ref[...] = _layer_norm(alpha * h_ref[...] + y, g_ref[...], b_ref[...])


def _combine(ys, dest, h, gates, ln_g, ln_b, alpha):
    n, d = h.shape
    tt = COMBINE_TOKENS
    assert n % tt == 0
    return pl.pallas_call(
        functools.partial(_combine_kernel, alpha=alpha),
        grid_spec=pltpu.PrefetchScalarGridSpec(
            num_scalar_prefetch=1, grid=(n // tt,),
            in_specs=[pl.BlockSpec(memory_space=pl.ANY),
                      pl.BlockSpec((tt, d), lambda i, dr: (i, 0)),
                      pl.BlockSpec((tt, TOP_K), lambda i, dr: (i, 0)),
                      pl.BlockSpec((1, d), lambda i, dr: (0, 0)),
                      pl.BlockSpec((1, d), lambda i, dr: (0, 0))],
            out_specs=pl.BlockSpec((tt, d), lambda i, dr: (i, 0)),
            scratch_shapes=[pltpu.VMEM((2, TOP_K, tt, d), F32),
                            pltpu.SemaphoreType.DMA((2,))]),
        out_shape=jax.ShapeDtypeStruct((n, d), F32),
        compiler_params=_cparams(("arbitrary",), 48),
        name="moe_combine",
    )(dest, ys, h, gates, ln_g.reshape(1, d), ln_b.reshape(1, d))


def _moe_layer(h, idx, gates, ln_g, ln_b, layer, w_gate, b_gate, w_up, b_up, w_down, b_down, alpha):
    tabs = _routing_tables(idx, w_gate.shape[1])
    xs = _dispatch(h, tabs)
    ys = _experts(xs, tabs, layer, w_gate, b_gate, w_up, b_up, w_down, b_down)
    return _combine(ys, tabs["dest"], h, gates, ln_g, ln_b, alpha)


def _proj_kernel(x_ref, w_ref, lb_ref, o_ref, *, tiles_per_section):
    sec = pl.program_id(1) // tiles_per_section
    lb = lb_ref[...]
    sub = x_ref.shape[0] // PROJ_SPLIT
    for c in range(PROJ_SPLIT):
        rows = slice(c * sub, (c + 1) * sub)
        p = jnp.dot(x_ref[rows, :], w_ref[...], preferred_element_type=F32)
        sig = jax.nn.sigmoid(p)
        forget = lb + (1.0 - lb) * sig
        o_ref[rows, :] = jnp.where(sec == 1, forget, jnp.where(sec == 2, p, p * sig))


def _hgrn_proj(h, w_in, lb):
    n, d = h.shape
    tn = 1024
    tm = 928
    assert n % tm == 0 and d % tn == 0 and w_in.shape[1] == 4 * d and tm % (8 * PROJ_SPLIT) == 0
    tps = d // tn
    return pl.pallas_call(
        functools.partial(_proj_kernel, tiles_per_section=tps),
        grid=(n // tm, 4 * tps),
        in_specs=[pl.BlockSpec((tm, d), lambda i, j: (i, 0)),
                  pl.BlockSpec((d, tn), lambda i, j: (0, j)),
                  pl.BlockSpec((1, tn), lambda i, j: (0, j % tps))],
        out_specs=pl.BlockSpec((tm, tn), lambda i, j: (i, j)),
        out_shape=jax.ShapeDtypeStruct((n, 4 * d), F32),
        compiler_params=_cparams(("arbitrary", "arbitrary"), 48),
        name="hgrn_proj",
    )(h, w_in, lb.reshape(1, d))


_NT = (((1,), (1,)), ((), ()))
_TN = (((0,), (0,)), ((), ()))


def _hgrn_block(q, f, v, st):
    c = q.shape[0]
    n_sub = c // SUB
    lf = jnp.log(f)
    k = 1.0 - f
    row = lax.broadcasted_iota(jnp.int32, (c, 1), 0)
    loc = row % SUB
    bl = lf
    for sh in (1, 2, 4, 8):
        bl = bl + jnp.where(loc >= sh, pltpu.roll(bl, sh, axis=0), 0.0)
    tot = [bl[SUB * i + SUB - 1:SUB * i + SUB, :] for i in range(n_sub)]
    off = [jnp.zeros_like(tot[0])]
    for i in range(1, n_sub):
        off.append(off[-1] + tot[i - 1])
    b_last = off[-1] + tot[-1]
    if n_sub > 1:
        blk = row // SUB
        offs = off[0]
        tots = tot[0]
        for i in range(1, n_sub):
            offs = jnp.where(blk == i, off[i], offs)
            tots = jnp.where(blk == i, tot[i], tots)
        b = bl + offs
    else:
        b = bl
        tots = tot[0]
    r2 = lax.broadcasted_iota(jnp.int32, (c, c), 0)
    c2 = lax.broadcasted_iota(jnp.int32, (c, c), 1)
    a_loc = lax.dot_general(q * jnp.exp(bl), k * jnp.exp(-bl), _NT, preferred_element_type=F32)
    a = jnp.where((r2 // SUB == c2 // SUB) & (c2 <= r2), a_loc, 0.0)
    if n_sub > 1:
        k_end = k * jnp.exp(tots - bl)
        for jb in range(n_sub - 1):
            gamma = off[jb] + tot[jb]
            q_rel = q * jnp.exp(jnp.minimum(b - gamma, 0.0))
            a_j = lax.dot_general(q_rel, k_end, _NT, preferred_element_type=F32)
            a = jnp.where((c2 // SUB == jb) & (r2 // SUB > jb), a_j, a)
    o = jnp.dot(a, v, preferred_element_type=F32)
    o = o + lax.dot_general(q * jnp.exp(b), st, _NT, preferred_element_type=F32)
    kd = k * jnp.exp(b_last - b)
    st_new = st * jnp.exp(b_last) + lax.dot_general(v, kd, _TN, preferred_element_type=F32)
    o = o * lax.rsqrt(jnp.mean(o * o, axis=-1, keepdims=True) + RMS_EPS)
    return o, st_new


def _rec_prompt_kernel(q_ref, f_ref, v_ref, o_ref, s_ref, st_scr, *, n_blocks):
    st_scr[...] = jnp.zeros_like(st_scr)

    def run(r0, c):
        rows = pl.ds(r0, c)
        for hd in range(REC_HEADS):
            cols = slice(hd * HG_DK, (hd + 1) * HG_DK)
            o, st = _hgrn_block(q_ref[rows, cols], f_ref[rows, cols], v_ref[rows, cols], st_scr[hd])
            o_ref[rows, cols] = o
            st_scr[hd] = st

    run(0, SUB)

    def body(i, carry):
        run(pl.multiple_of(SUB + i * REC_BLOCK, SUB), REC_BLOCK)
        return carry
    lax.fori_loop(0, n_blocks, body, 0, unroll=2)
    for hd in range(REC_HEADS):
        s_ref[0, hd] = st_scr[hd].T


def _rec_sample_kernel(q_ref, f_ref, v_ref, s0_ref, o_ref, s_ref, *, t_new):
    sb = s0_ref.shape[1]

    def body(i, carry):
        for u in range(REC_SEQS):
            sq = i * REC_SEQS + u
            rows = pl.ds(pl.multiple_of(sq * t_new, t_new), t_new)
            o, st = _hgrn_block_short(q_ref[rows, :], f_ref[rows, :], v_ref[rows, :],
                                      s0_ref[0, sq, 0].T)
            o_ref[rows, :] = o
            s_ref[0, sq, 0] = st.T
        return carry
    lax.fori_loop(0, sb // REC_SEQS, body, 0)


def _hgrn_block_short(q, f, v, st):
    c = q.shape[0]
    lf = jnp.log(f)
    k = 1.0 - f
    row = lax.broadcasted_iota(jnp.int32, (c, 1), 0)
    bl = lf
    sh = 1
    while sh < c:
        bl = bl + jnp.where(row >= sh, pltpu.roll(bl, sh, axis=0), 0.0)
        sh *= 2
    b_last = bl[c - 1:c, :]
    r2 = lax.broadcasted_iota(jnp.int32, (c, c), 0)
    c2 = lax.broadcasted_iota(jnp.int32, (c, c), 1)
    qe = q * jnp.exp(bl)
    a = lax.dot_general(qe, k * jnp.exp(-bl), _NT, preferred_element_type=F32)
    a = jnp.where(c2 <= r2, a, 0.0)
    o = jnp.dot(a, v, preferred_element_type=F32)
    o = o + lax.dot_general(qe, st, _NT, preferred_element_type=F32)
    kd = k * jnp.exp(b_last - bl)
    st_new = st * jnp.exp(b_last) + lax.dot_general(v, kd, _TN, preferred_element_type=F32)
    o = o * lax.rsqrt(jnp.mean(o * o, axis=-1, keepdims=True) + RMS_EPS)
    return o, st_new


def _hgrn_recurrence(proj, state_s, b_p, t_p, b_s, t_s):
    n, d4 = proj.shape
    d = d4 // 4
    heads = d // HG_DK
    n_p = b_p * t_p
    assert (t_p - SUB) % REC_BLOCK == 0 and t_s < SUB and t_s % 8 == 0
    assert heads % REC_HEADS == 0
    hgroups = heads // REC_HEADS
    gw = REC_HEADS * HG_DK
    o_p, s_p = pl.pallas_call(
        functools.partial(_rec_prompt_kernel, n_blocks=(t_p - SUB) // REC_BLOCK),
        grid=(b_p, hgroups),
        in_specs=[pl.BlockSpec((t_p, gw), lambda b, h: (b, h)),
                  pl.BlockSpec((t_p, gw), lambda b, h: (b, hgroups + h)),
                  pl.BlockSpec((t_p, gw), lambda b, h: (b, 2 * hgroups + h))],
        out_specs=[pl.BlockSpec((t_p, gw), lambda b, h: (b, h)),
                   pl.BlockSpec((1, REC_HEADS, HG_DK, HG_DK), lambda b, h: (b, h, 0, 0))],
        out_shape=[jax.ShapeDtypeStruct((n_p, d), F32),
                   jax.ShapeDtypeStruct((b_p, heads, HG_DK, HG_DK), F32)],
        scratch_shapes=[pltpu.VMEM((REC_HEADS, HG_DK, HG_DK), F32)],
        compiler_params=_cparams(("arbitrary", "arbitrary"), 48),
        name="hgrn_rec_prompt",
    )(proj, proj, proj)

    proj_s = proj[n_p:]
    sb = 32
    assert b_s % sb == 0
    o_s, s_s = pl.pallas_call(
        functools.partial(_rec_sample_kernel, t_new=t_s),
        grid=(heads, b_s // sb),
        in_specs=[pl.BlockSpec((sb * t_s, HG_DK), lambda h, i: (i, h)),
                  pl.BlockSpec((sb * t_s, HG_DK), lambda h, i: (i, heads + h)),
                  pl.BlockSpec((sb * t_s, HG_DK), lambda h, i: (i, 2 * heads + h)),
                  pl.BlockSpec((1, sb, 1, HG_DK, HG_DK), lambda h, i: (0, i, h, 0, 0))],
        out_specs=[pl.BlockSpec((sb * t_s, HG_DK), lambda h, i: (i, h)),
                   pl.BlockSpec((1, sb, 1, HG_DK, HG_DK), lambda h, i: (0, i, h, 0, 0))],
        out_shape=[jax.ShapeDtypeStruct((b_s * t_s, d), F32),
                   jax.ShapeDtypeStruct((1, b_s, heads, HG_DK, HG_DK), F32)],
        compiler_params=_cparams(("arbitrary", "arbitrary"), 32),
        name="hgrn_rec_sample",
    )(proj_s, proj_s, proj_s, state_s)
    return jnp.concatenate([o_p, o_s], axis=0), s_p, s_s


def _hgrn_out_kernel(o_ref, sg_ref, ng_ref, w_ref, x_ref, g_ref, b_ref, rw_ref, rb_ref,
                     h_ref, idx_ref, gate_ref, *, alpha):
    sub = o_ref.shape[0] // OUT_SPLIT
    for c in range(OUT_SPLIT):
        rows = slice(c * sub, (c + 1) * sub)
        z = (o_ref[rows, :] * ng_ref[...] * sg_ref[rows, :]).astype(BF16)
        y = jnp.dot(z, w_ref[...], preferred_element_type=F32)
        _mix_post(x_ref[rows, :], y, alpha, g_ref, b_ref, rw_ref, rb_ref, h_ref, idx_ref, gate_ref, rows)


def _hgrn_out(o, proj, norm_g, w_out, x, ln_g, ln_b, router_w, router_b, alpha):
    n, d = x.shape
    n_e = router_w.shape[1]
    tm = 464
    assert n % tm == 0 and tm % (8 * OUT_SPLIT) == 0
    consts = (ln_g.reshape(1, d), ln_b.reshape(1, d), router_w, router_b.reshape(1, n_e))
    row = lambda i: (i, 0)
    return pl.pallas_call(
        functools.partial(_hgrn_out_kernel, alpha=alpha),
        grid=(n // tm,),
        in_specs=[pl.BlockSpec((tm, d), row),
                  pl.BlockSpec((tm, d), lambda i: (i, 3)),
                  _const_spec((1, d)),
                  _const_spec((d, d)),
                  pl.BlockSpec((tm, d), row)] + [_const_spec(c.shape) for c in consts],
        out_specs=[pl.BlockSpec((tm, d), row), pl.BlockSpec((tm, TOP_K), row),
                   pl.BlockSpec((tm, TOP_K), row)],
        out_shape=[jax.ShapeDtypeStruct((n, d), F32),
                   jax.ShapeDtypeStruct((n, TOP_K), jnp.int32),
                   jax.ShapeDtypeStruct((n, TOP_K), F32)],
        compiler_params=_cparams(("arbitrary",), 56),
        name="hgrn_out",
    )(o, proj, norm_g.reshape(1, d), w_out.astype(BF16), x, *consts)


def kernel(x_prompt, x_sample, state_pool, state_hgrn, meta_tokens, pool_w, pool_scale, hg_w_in, hg_lb, hg_norm_g, hg_w_out, ln_g, ln_b, router_w, router_b, w_gate, b_gate, w_up, b_up, w_down, b_down):
    depth = ln_g.shape[0]
    assert depth == 2, "layer 0 = pooling mixer, layer 1 = HGRN2 mixer"
    alpha = float((2 * depth) ** 0.25)
    b_p, seq, d = x_prompt.shape
    b_s, t_s, _ = x_sample.shape
    t_p = seq + N_META
    n_p = b_p * t_p

    xp = jnp.concatenate([jnp.broadcast_to(meta_tokens[None], (b_p, N_META, d)), x_prompt], axis=1)
    prefix = state_pool[0]
    xs_ext = jnp.concatenate([jnp.zeros((b_s, 16 - POOL_BUF, d), F32), prefix, x_sample], axis=1)

    h, idx, gates = _pool_layer(xp, xs_ext, pool_w[0], pool_scale[0], ln_g[0, 0], ln_b[0, 0],
                                router_w[0], router_b[0], alpha)
    moe_w = (w_gate, b_gate, w_up, b_up, w_down, b_down)
    h = _moe_layer(h, idx, gates, ln_g[0, 1], ln_b[0, 1], 0, *moe_w, alpha)

    sm = jax.nn.softmax(hg_lb.astype(F32), axis=0)
    lb = (jnp.cumsum(sm, axis=0) - sm[0:1])[1]
    proj = _hgrn_proj(h, hg_w_in[0], lb)
    o, s_p, s_s = _hgrn_recurrence(proj, state_hgrn, b_p, t_p, b_s, t_s)
    h, idx, gates = _hgrn_out(o, proj, hg_norm_g[0], hg_w_out[0], h, ln_g[1, 0], ln_b[1, 0],
                              router_w[1], router_b[1], alpha)
    h = _moe_layer(h, idx, gates, ln_g[1, 1], ln_b[1, 1], 1, *moe_w, alpha)

    y_prompt = h[:n_p].reshape(b_p, t_p, d)[:, N_META:]
    y_sample = h[n_p:].reshape(b_s, t_s, d)
    pool_p = x_prompt[:, seq - POOL_BUF:][None]
    pool_s = jnp.concatenate([prefix, x_sample], axis=1)[:, t_s:][None]
    return (y_prompt, y_sample, pool_p, pool_s, s_p[None], s_s)
```
